```python
import math
import jax, jax.numpy as jnp
from jax import lax
import numpy as np

D_MODEL = 1024
BATCH = 16
SEQ = 4096
DEPTH = 2

N_HEADS = 16
HEAD_DIM = D_MODEL // N_HEADS
D_FF = ((8 * D_MODEL // 3) + 255) // 256 * 256
CONV_WIDTH = 3
DILATED_BRANCHES = ((128, 1), (512, 4), (2048, 16))
BLOCK = 128
REL_BUCKETS = 32
REL_MAX_DISTANCE = 2048
N_A_LAYERS = DEPTH // 2
N_B_LAYERS = DEPTH - N_A_LAYERS
RMS_EPS = 1e-6

kernel_name = "yoco_shortconv_dilated_attention_trunk"


def rmsnorm(x, g):
    xf = x.astype(jnp.float32)
    y = xf * lax.rsqrt(jnp.mean(xf * xf, axis=-1, keepdims=True) + RMS_EPS)
    return (y * g.astype(jnp.float32)).astype(x.dtype)


def causal_dwconv(x, w, b=None):
    S = x.shape[1]
    xp = jnp.pad(x, ((0, 0), (CONV_WIDTH - 1, 0), (0, 0)))
    y = xp[:, 0:S] * w[0]
    for tap in range(1, CONV_WIDTH):
        y = y + xp[:, tap:tap + S] * w[tap]
    if b is not None:
        y = y + b
    return y


def t5_bucket(dist):
    max_exact = REL_BUCKETS // 2
    n = jnp.maximum(dist, 0)
    nf = jnp.maximum(n, max_exact).astype(jnp.float32)
    large = max_exact + (jnp.log(nf / max_exact) / math.log(REL_MAX_DISTANCE / max_exact)
                         * (REL_BUCKETS - max_exact)).astype(jnp.int32)
    large = jnp.minimum(large, REL_BUCKETS - 1)
    return jnp.where(n < max_exact, n, large)


def short_conv_mixer(xn, w_in, conv_w, w_out):
    b_gate, c_gate, h = jnp.split(xn @ w_in, 3, axis=-1)
    return (b_gate * causal_dwconv(c_gate * h, conv_w)) @ w_out


def conv_ffn(xn, w_up, conv_w, conv_b, w_down):
    u = causal_dwconv(xn @ w_up, conv_w, conv_b)
    g, up = jnp.split(u, 2, axis=-1)
    return (jax.nn.silu(g) * up) @ w_down


def dilated_branch(q, k, v, rel_bias, window, dilation):
    B, S, H, Dh = q.shape
    P = BLOCK
    W = window // dilation
    L = S // dilation
    Lp = -(-L // P) * P
    nb = Lp // P

    def to_sub(t):
        return t.reshape(B, L, dilation, H, Dh).transpose(2, 0, 1, 3, 4)

    def to_blocks(t):
        return (t.reshape(dilation, B, nb, P, H, Dh).transpose(0, 2, 1, 3, 4, 5)
                .reshape(dilation * nb, B, P, H, Dh))

    qs = jnp.pad(to_sub(q), ((0, 0), (0, 0), (0, Lp - L), (0, 0), (0, 0)))
    ks = jnp.pad(to_sub(k), ((0, 0), (0, 0), (P, Lp - L), (0, 0), (0, 0)))
    vs = jnp.pad(to_sub(v), ((0, 0), (0, 0), (P, Lp - L), (0, 0), (0, 0)))
    q_blk = to_blocks(qs)
    k_prev, k_cur = to_blocks(ks[:, :, :Lp]), to_blocks(ks[:, :, P:])
    v_prev, v_cur = to_blocks(vs[:, :, :Lp]), to_blocks(vs[:, :, P:])
    blk_idx = jnp.tile(jnp.arange(nb, dtype=jnp.int32), dilation)

    qi = jnp.arange(P, dtype=jnp.int32)[:, None]
    kc = jnp.arange(2 * P, dtype=jnp.int32)[None, :]
    delta = qi + P - kc
    band = (delta >= 0) & (delta <= W)
    bias = rel_bias[t5_bucket(delta * dilation)].astype(jnp.float32).transpose(2, 0, 1)
    scale = HEAD_DIM ** -0.5

    def block_fn(args):
        qb, kp, kcur, vp, vcur, j = args
        kw = jnp.concatenate([kp, kcur], axis=1).astype(jnp.float32)
        vw = jnp.concatenate([vp, vcur], axis=1).astype(jnp.float32)
        s = jnp.einsum('bqhd,bkhd->bhqk', qb.astype(jnp.float32), kw) * scale + bias
        valid = band & ((j * P + kc - P) >= 0)
        s = jnp.where(valid, s, -jnp.inf)
        m = jnp.max(s, axis=-1)
        p = jnp.exp(s - m[..., None])
        den = jnp.sum(p, axis=-1)
        num = jnp.einsum('bhqk,bkhd->bqhd', p, vw)
        return num, den.transpose(0, 2, 1), m.transpose(0, 2, 1)

    num, den, mx = lax.map(block_fn, (q_blk, k_prev, k_cur, v_prev, v_cur, blk_idx))

    def from_blocks(t):
        t = t.reshape((dilation, nb, B, P) + t.shape[3:])
        t = jnp.moveaxis(jnp.moveaxis(t, 0, 3), 0, 1)
        t = t.reshape((B, Lp, dilation) + t.shape[4:])[:, :L]
        return t.reshape((B, S) + t.shape[3:])

    return from_blocks(num), from_blocks(den), from_blocks(mx)


def dilated_attention(xn, w_q, w_o, k, v, rel_bias):
    B, S, _ = xn.shape
    q = (xn @ w_q).reshape(B, S, N_HEADS, HEAD_DIM)
    branches = [dilated_branch(q, k, v, rel_bias, w, d) for (w, d) in DILATED_BRANCHES]
    m_all = jnp.max(jnp.stack([br[2] for br in branches]), axis=0)
    num_tot = jnp.zeros(q.shape, jnp.float32)
    den_tot = jnp.zeros(m_all.shape, jnp.float32)
    for num, den, mx in branches:
        wgt = jnp.exp(mx - m_all)
        num_tot = num_tot + wgt[..., None] * num
        den_tot = den_tot + wgt * den
    out = (num_tot / den_tot[..., None]).astype(xn.dtype).reshape(B, S, D_MODEL)
    return out @ w_o


def setup_inputs(seed: int = 0) -> dict:
    key = jax.random.key(seed)
    ks = jax.random.split(key, 17)
    f32 = jnp.float32
    D, F = D_MODEL, D_FF

    def nrm(k, shape, scale):
        return jax.random.normal(k, shape, f32) * scale

    def gain(k, shape):
        return 1.0 + 0.02 * jax.random.normal(k, shape, f32)

    return {
        "x": nrm(ks[0], (BATCH, SEQ, D), 1.0),
        "a_norm": gain(ks[1], (N_A_LAYERS, D)),
        "a_w_in": nrm(ks[2], (N_A_LAYERS, D, 3 * D), D ** -0.5),
        "a_conv": nrm(ks[3], (N_A_LAYERS, CONV_WIDTH, D), CONV_WIDTH ** -0.5),
        "a_w_out": nrm(ks[4], (N_A_LAYERS, D, D), D ** -0.5),
        "kv_norm": gain(ks[5], (D,)),
        "w_kv": nrm(ks[6], (D, 2 * D), D ** -0.5),
        "b_norm": gain(ks[7], (N_B_LAYERS, D)),
        "b_w_q": nrm(ks[8], (N_B_LAYERS, D, D), D ** -0.5),
        "b_w_o": nrm(ks[9], (N_B_LAYERS, D, D), D ** -0.5),
        "rel_bias": nrm(ks[10], (REL_BUCKETS, N_HEADS), 0.5),
        "ffn_norm": gain(ks[11], (DEPTH, D)),
        "ffn_w_up": nrm(ks[12], (DEPTH, D, 2 * F), D ** -0.5),
        "ffn_conv": nrm(ks[13], (DEPTH, CONV_WIDTH, 2 * F), CONV_WIDTH ** -0.5),
        "ffn_conv_b": nrm(ks[14], (DEPTH, 2 * F), 0.02),
        "ffn_w_down": nrm(ks[15], (DEPTH, F, D), F ** -0.5),
        "final_norm": gain(ks[16], (D,)),
    }


def reference(x, a_norm, a_w_in, a_conv, a_w_out, kv_norm, w_kv, b_norm, b_w_q, b_w_o, rel_bias,
              ffn_norm, ffn_w_up, ffn_conv, ffn_conv_b, ffn_w_down, final_norm):
    B, S, _ = x.shape
    h = x
    k = v = None
    for l in range(DEPTH):
        if l < N_A_LAYERS:
            h = h + short_conv_mixer(rmsnorm(h, a_norm[l]), a_w_in[l], a_conv[l], a_w_out[l])
        else:
            j = l - N_A_LAYERS
            h = h + dilated_attention(rmsnorm(h, b_norm[j]), b_w_q[j], b_w_o[j], k, v, rel_bias)
        h = h + conv_ffn(rmsnorm(h, ffn_norm[l]), ffn_w_up[l], ffn_conv[l], ffn_conv_b[l], ffn_w_down[l])
        if l == N_A_LAYERS - 1:
            k_flat, v_flat = jnp.split(rmsnorm(h, kv_norm) @ w_kv, 2, axis=-1)
            k = k_flat.reshape(B, S, N_HEADS, HEAD_DIM)
            v = v_flat.reshape(B, S, N_HEADS, HEAD_DIM)
    return rmsnorm(h, final_norm)
```

```python
import functools
import math

import jax
import jax.numpy as jnp
from jax import lax
from jax.experimental import pallas as pl
from jax.experimental.pallas import tpu as pltpu

N_HEADS = 16
CONV_WIDTH = 3
DILATED_BRANCHES = ((128, 1), (512, 4), (2048, 16))
BLOCK = 128
REL_BUCKETS = 32
REL_MAX_DISTANCE = 2048
RMS_EPS = 1e-6

LANES = 128
SUBLANES = 8
ROW_TILE = 512
COL_CHUNK = 256
VMEM_LIMIT = 56 * 1024 * 1024
MASKED = -1e30

BF16 = jnp.bfloat16
F32 = jnp.float32


def _dot(a, b):
    return jnp.dot(a, b, preferred_element_type=F32)


def _rms_scale(x):
    return x * lax.rsqrt(jnp.mean(x * x, axis=-1, keepdims=True) + RMS_EPS)


def _shift_rows(cur, prev, k):
    rolled = pltpu.roll(cur, k, 0)
    top = pltpu.roll(prev, k, 0)
    row = lax.broadcasted_iota(jnp.int32, prev.shape, 0)
    first = jnp.where(row < k, top, rolled[:SUBLANES])
    return jnp.concatenate([first, rolled[SUBLANES:]], axis=0)


def _causal_conv3(u, prev, w_ref, cols):
    u1 = _shift_rows(u, prev, 1)
    u2 = _shift_rows(u, prev, 2)
    return w_ref[0:1, cols] * u2 + w_ref[1:2, cols] * u1 + w_ref[2:3, cols] * u


def _resident(shape):
    return pl.BlockSpec(shape, lambda b, i: (0,) * len(shape), pipeline_mode=pl.Buffered(1))


def _row_spec(tm, width):
    return pl.BlockSpec((1, tm, width), lambda b, i: (b, i, 0))


_DENSE_PARAMS = pltpu.CompilerParams(
    dimension_semantics=("arbitrary", "arbitrary"), vmem_limit_bytes=VMEM_LIMIT)


def _mixer_kernel(x_ref, g_ref, win_ref, cw_ref, wout_ref, o_ref, carry_ref, y_ref):
    tm, d = x_ref.shape[1], x_ref.shape[2]
    x = x_ref[0]
    xn = (_rms_scale(x) * g_ref[...]).astype(BF16)

    @pl.when(pl.program_id(1) == 0)
    def _():
        carry_ref[...] = jnp.zeros_like(carry_ref)

    for j in range(d // COL_CHUNK):
        cols = slice(j * COL_CHUNK, (j + 1) * COL_CHUNK)
        gate_b = _dot(xn, win_ref[:, cols])
        gate_c = _dot(xn, win_ref[:, d + j * COL_CHUNK:d + (j + 1) * COL_CHUNK])
        hh = _dot(xn, win_ref[:, 2 * d + j * COL_CHUNK:2 * d + (j + 1) * COL_CHUNK])
        ch = gate_c * hh
        conv = _causal_conv3(ch, carry_ref[:, cols], cw_ref, cols)
        carry_ref[:, cols] = ch[tm - SUBLANES:]
        y_ref[:, cols] = (gate_b * conv).astype(BF16)
    o_ref[0] = x + _dot(y_ref[...], wout_ref[...])


def _mixer(x, gain, w_in, conv_w, w_out):
    b, s, d = x.shape
    tm = ROW_TILE
    return pl.pallas_call(
        _mixer_kernel,
        grid=(b, s // tm),
        in_specs=[_row_spec(tm, d), _resident((1, d)), _resident((d, 3 * d)),
                  _resident((CONV_WIDTH, d)), _resident((d, d))],
        out_specs=_row_spec(tm, d),
        out_shape=jax.ShapeDtypeStruct((b, s, d), F32),
        scratch_shapes=[pltpu.VMEM((SUBLANES, d), F32), pltpu.VMEM((tm, d), BF16)],
        compiler_params=_DENSE_PARAMS,
        name="mixer",
    )(x, gain.reshape(1, d), w_in.astype(BF16), conv_w, w_out.astype(BF16))


def _ffn_kernel(*refs, has_attn, has_final):
    refs = list(refs)
    x_ref = refs.pop(0)
    att_ref, wo_ref = (refs.pop(0), refs.pop(0)) if has_attn else (None, None)
    g_ref, wup_ref, cw_ref, cb_ref, wdn_ref = (refs.pop(0) for _ in range(5))
    fg_ref = refs.pop(0) if has_final else None
    o_ref, carry_ref, act_ref = refs
    tm = x_ref.shape[1]
    f = wdn_ref.shape[0]

    x = x_ref[0]
    if has_attn:
        x = x + _dot(att_ref[0], wo_ref[...])
    xn = (_rms_scale(x) * g_ref[...]).astype(BF16)

    @pl.when(pl.program_id(1) == 0)
    def _():
        carry_ref[...] = jnp.zeros_like(carry_ref)

    def conv_chunk(c0):
        cols = slice(c0, c0 + COL_CHUNK)
        u = _dot(xn, wup_ref[:, cols])
        out = _causal_conv3(u, carry_ref[:, cols], cw_ref, cols) + cb_ref[:, cols]
        carry_ref[:, cols] = u[tm - SUBLANES:]
        return out

    for j in range(f // COL_CHUNK):
        gate = conv_chunk(j * COL_CHUNK)
        up = conv_chunk(f + j * COL_CHUNK)
        act = gate / (1.0 + jnp.exp(-gate)) * up
        act_ref[:, j * COL_CHUNK:(j + 1) * COL_CHUNK] = act.astype(BF16)
    y = x + _dot(act_ref[...], wdn_ref[...])
    if has_final:
        y = _rms_scale(y) * fg_ref[...]
    o_ref[0] = y


def _ffn(x, gain, w_up, conv_w, conv_b, w_down, att=None, w_o=None, final_gain=None):
    b, s, d = x.shape
    f = w_down.shape[0]
    tm = ROW_TILE
    has_attn, has_final = att is not None, final_gain is not None
    args, specs = [x], [_row_spec(tm, d)]
    if has_attn:
        args += [att, w_o.astype(BF16)]
        specs += [_row_spec(tm, d), _resident((d, d))]
    args += [gain.reshape(1, d), w_up.astype(BF16), conv_w, conv_b.reshape(1, 2 * f), w_down.astype(BF16)]
    specs += [_resident((1, d)), _resident((d, 2 * f)), _resident((CONV_WIDTH, 2 * f)),
              _resident((1, 2 * f)), _resident((f, d))]
    if has_final:
        args.append(final_gain.reshape(1, d))
        specs.append(_resident((1, d)))
    return pl.pallas_call(
        functools.partial(_ffn_kernel, has_attn=has_attn, has_final=has_final),
        grid=(b, s // tm),
        in_specs=specs,
        out_specs=_row_spec(tm, d),
        out_shape=jax.ShapeDtypeStruct((b, s, d), F32),
        scratch_shapes=[pltpu.VMEM((SUBLANES, 2 * f), F32), pltpu.VMEM((tm, f), BF16)],
        compiler_params=_DENSE_PARAMS,
        name="ffn_attn" if has_attn else "ffn",
    )(*args)


def _kvq_kernel(x_ref, gkv_ref, gq_ref, wkv_ref, wq_ref, k_ref, v_ref, q_ref, *, scale):
    d = x_ref.shape[2]
    y = _rms_scale(x_ref[0])
    kv = _dot((y * gkv_ref[...]).astype(BF16), wkv_ref[...])
    k_ref[0] = kv[:, :d].astype(BF16)
    v_ref[0] = kv[:, d:].astype(BF16)
    q_ref[0] = (_dot((y * gq_ref[...]).astype(BF16), wq_ref[...]) * scale).astype(BF16)


def _kvq(x, kv_gain, q_gain, w_kv, w_q, scale):
    b, s, d = x.shape
    tm = ROW_TILE
    out = jax.ShapeDtypeStruct((b, s, d), BF16)
    return pl.pallas_call(
        functools.partial(_kvq_kernel, scale=scale),
        grid=(b, s // tm),
        in_specs=[_row_spec(tm, d), _resident((1, d)), _resident((1, d)),
                  _resident((d, 2 * d)), _resident((d, d))],
        out_specs=[_row_spec(tm, d)] * 3,
        out_shape=[out] * 3,
        compiler_params=_DENSE_PARAMS,
        name="kvq",
    )(x, kv_gain.reshape(1, d), q_gain.reshape(1, d), w_kv.astype(BF16), w_q.astype(BF16))


def _t5_bucket(dist):
    max_exact = REL_BUCKETS // 2
    n = jnp.maximum(dist, 0)
    nf = jnp.maximum(n, max_exact).astype(jnp.float32)
    large = max_exact + (jnp.log(nf / max_exact) / math.log(REL_MAX_DISTANCE / max_exact)
                         * (REL_BUCKETS - max_exact)).astype(jnp.int32)
    large = jnp.minimum(large, REL_BUCKETS - 1)
    return jnp.where(n < max_exact, n, large)


def _bias_tables(rel_bias):
    p = BLOCK
    qi = jnp.arange(p, dtype=jnp.int32)[:, None]
    kc = jnp.arange(2 * p, dtype=jnp.int32)[None, :]
    delta = qi + p - kc
    tables = []
    for window, dilation in DILATED_BRANCHES:
        band = (delta >= 0) & (delta <= window // dilation)
        bias = rel_bias[_t5_bucket(delta * dilation)].astype(F32).transpose(2, 0, 1)
        inner = jnp.where(band, bias, MASKED)
        first = jnp.where(band & (kc >= p), bias, MASKED)
        tables.append(jnp.stack([inner, first], axis=0))
    t = jnp.stack(tables, axis=0)
    t = t.reshape(len(DILATED_BRANCHES), 2, N_HEADS // 2, 2, p, 2 * p)
    return t.transpose(2, 0, 1, 3, 4, 5)


def _attn_kernel(q_ref, k_ref, v_ref, bm_ref, o_ref, stage_ref, qd_ref, kp_ref, va_ref, vb_ref,
                 acc_ref, den_ref, mx_ref):
    s = q_ref.shape[1]
    p = BLOCK
    dils = tuple(d for _, d in DILATED_BRANCHES)
    head0 = lax.broadcasted_iota(jnp.int32, (1, LANES), 1) < LANES // 2

    def deinterleave(src_f32_ref, d, write):
        n = s // d
        for r in range(d):
            write(r * n, src_f32_ref[pl.ds(r, n, stride=d), :])

    zero_blk = jnp.zeros((p, LANES), BF16)
    for bi in range(len(dils)):
        kp_ref[bi, 0:p, :] = zero_blk
        va_ref[bi, 0:p, :] = zero_blk
        vb_ref[bi, 0:p, :] = zero_blk

    def put_k(bi):
        def write(off, rows):
            kp_ref[bi, p + off:p + off + rows.shape[0], :] = rows.astype(BF16)
        return write

    def put_v(bi):
        def write(off, rows):
            va_ref[bi, p + off:p + off + rows.shape[0], :] = jnp.where(head0, rows, 1.0).astype(BF16)
            vb_ref[bi, p + off:p + off + rows.shape[0], :] = jnp.where(head0, 1.0, rows).astype(BF16)
        return write

    def put_q(bi):
        def write(off, rows):
            qd_ref[bi, off:off + rows.shape[0], :] = rows.astype(BF16)
        return write

    for src_ref, put in ((k_ref, put_k), (v_ref, put_v), (q_ref, put_q)):
        stage_ref[...] = src_ref[0].astype(F32)
        for bi, d in enumerate(dils):
            deinterleave(stage_ref, d, put(bi))

    for bi, d in enumerate(dils):
        nb = (s // d) // p

        def block_body(g, carry, bi=bi, d=d, nb=nb):
            row0 = pl.multiple_of(g * p, p)
            qb = qd_ref[bi, pl.ds(row0, p), :]
            kw = kp_ref[bi, pl.ds(row0, 2 * p), :]
            first = jnp.where((g & (nb - 1)) == 0, 1, 0)
            outs, maxes = [], []
            for hh in range(2):
                qh = jnp.where(head0 if hh == 0 else jnp.logical_not(head0), qb, jnp.zeros_like(qb))
                sc = lax.dot_general(qh, kw, (((1,), (1,)), ((), ())), preferred_element_type=F32)
                sc = sc + bm_ref[0, bi, first, hh]
                m = jnp.max(sc, axis=-1, keepdims=True)
                pr = jnp.exp(sc - m).astype(BF16)
                vv = (va_ref if hh == 0 else vb_ref)[bi, pl.ds(row0, 2 * p), :]
                outs.append(_dot(pr, vv))
                maxes.append(m)
            num = jnp.where(head0, outs[0], outs[1])
            den_swapped = jnp.where(head0, outs[1], outs[0])
            mx = jnp.where(head0, maxes[0], maxes[1])
            start = (g & (nb - 1)) * (p * d) + (g // nb)
            rows = pl.ds(start, p, stride=d) if d > 1 else pl.ds(start, p)
            acc_ref[bi, rows, :] = num
            den_ref[bi, rows, :] = den_swapped
            mx_ref[bi, rows, :] = jnp.broadcast_to(mx, (p, LANES))
            return carry

        lax.fori_loop(0, s // p, block_body, 0)

    mt = 2 * p

    def merge_body(c, carry):
        rows = pl.ds(pl.multiple_of(c * mt, mt), mt)
        ms = [mx_ref[bi, rows, :] for bi in range(len(dils))]
        m_all = functools.reduce(jnp.maximum, ms)
        num = jnp.zeros((mt, LANES), F32)
        den_swapped = jnp.zeros((mt, LANES), F32)
        for bi in range(len(dils)):
            w = jnp.exp(ms[bi] - m_all)
            num = num + w * acc_ref[bi, rows, :]
            den_swapped = den_swapped + pltpu.roll(w, LANES // 2, 1) * den_ref[bi, rows, :]
        o_ref[0, rows, :] = (num / pltpu.roll(den_swapped, LANES // 2, 1)).astype(o_ref.dtype)
        return carry

    lax.fori_loop(0, s // mt, merge_body, 0)


def _attention(q, k, v, tables):
    b, s, d = q.shape
    nbr = len(DILATED_BRANCHES)
    qkv_spec = pl.BlockSpec((1, s, LANES), lambda hp, bb: (bb, 0, hp))
    tab_spec = pl.BlockSpec((1,) + tables.shape[1:], lambda hp, bb: (hp, 0, 0, 0, 0, 0))
    return pl.pallas_call(
        _attn_kernel,
        grid=(d // LANES, b),
        in_specs=[qkv_spec, qkv_spec, qkv_spec, tab_spec],
        out_specs=qkv_spec,
        out_shape=jax.ShapeDtypeStruct((b, s, d), BF16),
        scratch_shapes=[
            pltpu.VMEM((s, LANES), F32),
            pltpu.VMEM((nbr, s, LANES), BF16),
            pltpu.VMEM((nbr, s + BLOCK, LANES), BF16),
            pltpu.VMEM((nbr, s + BLOCK, LANES), BF16),
            pltpu.VMEM((nbr, s + BLOCK, LANES), BF16),
            pltpu.VMEM((nbr, s, LANES), F32),
            pltpu.VMEM((nbr, s, LANES), F32),
            pltpu.VMEM((nbr, s, LANES), F32),
        ],
        compiler_params=pltpu.CompilerParams(
            dimension_semantics=("arbitrary", "arbitrary"), vmem_limit_bytes=VMEM_LIMIT),
        name="dilated_attn",
    )(q, k, v, tables)


def kernel(x, a_norm, a_w_in, a_conv, a_w_out, kv_norm, w_kv, b_norm, b_w_q, b_w_o, rel_bias,
           ffn_norm, ffn_w_up, ffn_conv, ffn_conv_b, ffn_w_down, final_norm):
    d = x.shape[-1]
    n_a, n_b = a_norm.shape[0], b_norm.shape[0]
    depth = n_a + n_b
    assert n_b == 1, "the shared K/V and the query projection are fused for a single attention layer"
    scale = (d // N_HEADS) ** -0.5
    tables = _bias_tables(rel_bias)

    h = x
    for l in range(n_a):
        h = _mixer(h, a_norm[l], a_w_in[l], a_conv[l], a_w_out[l])
        h = _ffn(h, ffn_norm[l], ffn_w_up[l], ffn_conv[l], ffn_conv_b[l], ffn_w_down[l])
    k, v, q = _kvq(h, kv_norm, b_norm[0], w_kv, b_w_q[0], scale)
    att = _attention(q, k, v, tables)
    l = depth - 1
    return _ffn(h, ffn_norm[l], ffn_w_up[l], ffn_conv[l], ffn_conv_b[l], ffn_w_down[l],
                att=att, w_o=b_w_o[0], final_gain=final_norm)
```

```python
import functools
import math

import jax
import jax.numpy as jnp
from jax import lax
from jax.experimental import pallas as pl
from jax.experimental.pallas import tpu as pltpu

N_HEADS = 16
CONV_WIDTH = 3
DILATED_BRANCHES = ((128, 1), (512, 4), (2048, 16))
BLOCK = 128
REL_BUCKETS = 32
REL_MAX_DISTANCE = 2048
RMS_EPS = 1e-6

LANES = 128
SUBLANES = 8
ROW_TILE = 512
COL_CHUNK = 256
VMEM_LIMIT = 56 * 1024 * 1024
MASKED = -1e30

BF16 = jnp.bfloat16
F32 = jnp.float32


def _dot(a, b):
    return jnp.dot(a, b, preferred_element_type=F32)


def _rms_scale(x):
    return x * lax.rsqrt(jnp.mean(x * x, axis=-1, keepdims=True) + RMS_EPS)


def _shift_rows(cur, prev, k):
    rolled = pltpu.roll(cur, k, 0)
    top = pltpu.roll(prev, k, 0)
    row = lax.broadcasted_iota(jnp.int32, prev.shape, 0)
    first = jnp.where(row < k, top, rolled[:SUBLANES])
    return jnp.concatenate([first, rolled[SUBLANES:]], axis=0)


def _causal_conv3(u, prev, w_ref, cols):
    u1 = _shift_rows(u, prev, 1)
    u2 = _shift_rows(u, prev, 2)
    return w_ref[0:1, cols] * u2 + w_ref[1:2, cols] * u1 + w_ref[2:3, cols] * u


def _resident(shape):
    return pl.BlockSpec(shape, lambda b, i: (0,) * len(shape), pipeline_mode=pl.Buffered(1))


def _row_spec(tm, width):
    return pl.BlockSpec((1, tm, width), lambda b, i: (b, i, 0))


_DENSE_PARAMS = pltpu.CompilerParams(
    dimension_semantics=("arbitrary", "arbitrary"), vmem_limit_bytes=VMEM_LIMIT)


def _mixer_kernel(x_ref, g_ref, win_ref, cw_ref, wout_ref, o_ref, carry_ref, y_ref):
    tm, d = x_ref.shape[1], x_ref.shape[2]
    x = x_ref[0]
    xn = (_rms_scale(x) * g_ref[...]).astype(BF16)

    @pl.when(pl.program_id(1) == 0)
    def _():
        carry_ref[...] = jnp.zeros_like(carry_ref)

    for j in range(d // COL_CHUNK):
        cols = slice(j * COL_CHUNK, (j + 1) * COL_CHUNK)
        gate_b = _dot(xn, win_ref[:, cols])
        gate_c = _dot(xn, win_ref[:, d + j * COL_CHUNK:d + (j + 1) * COL_CHUNK])
        hh = _dot(xn, win_ref[:, 2 * d + j * COL_CHUNK:2 * d + (j + 1) * COL_CHUNK])
        ch = gate_c * hh
        conv = _causal_conv3(ch, carry_ref[:, cols], cw_ref, cols)
        carry_ref[:, cols] = ch[tm - SUBLANES:]
        y_ref[:, cols] = (gate_b * conv).astype(BF16)
    o_ref[0] = x + _dot(y_ref[...], wout_ref[...])


def _mixer(x, gain, w_in, conv_w, w_out):
    b, s, d = x.shape
    tm = ROW_TILE
    return pl.pallas_call(
        _mixer_kernel,
        grid=(b, s // tm),
        in_specs=[_row_spec(tm, d), _resident((1, d)), _resident((d, 3 * d)),
                  _resident((CONV_WIDTH, d)), _resident((d, d))],
        out_specs=_row_spec(tm, d),
        out_shape=jax.ShapeDtypeStruct((b, s, d), F32),
        scratch_shapes=[pltpu.VMEM((SUBLANES, d), F32), pltpu.VMEM((tm, d), BF16)],
        compiler_params=_DENSE_PARAMS,
        name="mixer",
    )(x, gain.reshape(1, d), w_in.astype(BF16), conv_w, w_out.astype(BF16))


def _ffn_kernel(*refs, has_attn, has_final):
    refs = list(refs)
    x_ref = refs.pop(0)
    att_ref, wo_ref = (refs.pop(0), refs.pop(0)) if has_attn else (None, None)
    g_ref, wup_ref, cw_ref, cb_ref, wdn_ref = (refs.pop(0) for _ in range(5))
    fg_ref = refs.pop(0) if has_final else None
    o_ref, carry_ref, act_ref = refs
    tm = x_ref.shape[1]
    f = wdn_ref.shape[0]

    x = x_ref[0]
    if has_attn:
        x = x + _dot(att_ref[0], wo_ref[...])
    xn = (_rms_scale(x) * g_ref[...]).astype(BF16)

    @pl.when(pl.program_id(1) == 0)
    def _():
        carry_ref[...] = jnp.zeros_like(carry_ref)

    def conv_chunk(c0):
        cols = slice(c0, c0 + COL_CHUNK)
        u = _dot(xn, wup_ref[:, cols])
        out = _causal_conv3(u, carry_ref[:, cols], cw_ref, cols) + cb_ref[:, cols]
        carry_ref[:, cols] = u[tm - SUBLANES:]
        return out

    for j in range(f // COL_CHUNK):
        gate = conv_chunk(j * COL_CHUNK)
        up = conv_chunk(f + j * COL_CHUNK)
        act = gate / (1.0 + jnp.exp(-gate)) * up
        act_ref[:, j * COL_CHUNK:(j + 1) * COL_CHUNK] = act.astype(BF16)
    y = x + _dot(act_ref[...], wdn_ref[...])
    if has_final:
        y = _rms_scale(y) * fg_ref[...]
    o_ref[0] = y


def _ffn(x, gain, w_up, conv_w, conv_b, w_down, att=None, w_o=None, final_gain=None):
    b, s, d = x.shape
    f = w_down.shape[0]
    tm = ROW_TILE
    has_attn, has_final = att is not None, final_gain is not None
    args, specs = [x], [_row_spec(tm, d)]
    if has_attn:
        args += [att, w_o.astype(BF16)]
        specs += [_row_spec(tm, d), _resident((d, d))]
    args += [gain.reshape(1, d), w_up.astype(BF16), conv_w, conv_b.reshape(1, 2 * f), w_down.astype(BF16)]
    specs += [_resident((1, d)), _resident((d, 2 * f)), _resident((CONV_WIDTH, 2 * f)),
              _resident((1, 2 * f)), _resident((f, d))]
    if has_final:
        args.append(final_gain.reshape(1, d))
        specs.append(_resident((1, d)))
    return pl.pallas_call(
        functools.partial(_ffn_kernel, has_attn=has_attn, has_final=has_final),
        grid=(b, s // tm),
        in_specs=specs,
        out_specs=_row_spec(tm, d),
        out_shape=jax.ShapeDtypeStruct((b, s, d), F32),
        scratch_shapes=[pltpu.VMEM((SUBLANES, 2 * f), F32), pltpu.VMEM((tm, f), BF16)],
        compiler_params=_DENSE_PARAMS,
        name="ffn_attn" if has_attn else "ffn",
    )(*args)


DMAJ = 16


def _kvq_kernel(x_ref, gkv_ref, gq_ref, wkv_ref, wq_ref, *refs, scale):
    (q_nat, k_nat, va_nat, vb_nat), (q_maj, k_maj, va_maj, vb_maj) = refs[0:4], refs[4:8]
    tok_ref, mid_ref = refs[8:10]
    tm, d = x_ref.shape[1], x_ref.shape[2]
    step = 4
    y = _rms_scale(x_ref[0])
    xkv = (y * gkv_ref[...]).astype(BF16)
    xq = (y * gq_ref[...]).astype(BF16)
    head0 = lax.broadcasted_iota(jnp.int32, (1, LANES), 1) < LANES // 2
    slab = [0]

    def emit(val, lanes, nat_ref, maj_ref):
        nat_ref[0, :, lanes] = val.astype(BF16)
        i = slab[0] = (slab[0] + 1) % tok_ref.shape[0]
        tok_ref[i] = val
        for r in range(step):
            mid_ref[i, r * (tm // step):(r + 1) * (tm // step), :] = tok_ref[i, pl.ds(r, tm // step, stride=step), :]
        for r in range(DMAJ):
            rows = pl.ds((r % step) * (tm // step) + r // step, tm // DMAJ, stride=step)
            maj_ref[0, r, :, lanes] = mid_ref[i, rows, :].astype(BF16)

    for j in range(d // COL_CHUNK):
        qc = _dot(xq, wq_ref[:, j * COL_CHUNK:(j + 1) * COL_CHUNK]) * scale
        kc = _dot(xkv, wkv_ref[:, j * COL_CHUNK:(j + 1) * COL_CHUNK])
        vc = _dot(xkv, wkv_ref[:, d + j * COL_CHUNK:d + (j + 1) * COL_CHUNK])
        for h in range(COL_CHUNK // LANES):
            lanes = slice(j * COL_CHUNK + h * LANES, j * COL_CHUNK + (h + 1) * LANES)
            part = slice(h * LANES, (h + 1) * LANES)
            emit(qc[:, part], lanes, q_nat, q_maj)
            emit(kc[:, part], lanes, k_nat, k_maj)
            emit(jnp.where(head0, vc[:, part], 1.0), lanes, va_nat, va_maj)
            emit(jnp.where(head0, 1.0, vc[:, part]), lanes, vb_nat, vb_maj)


def _kvq(x, kv_gain, q_gain, w_kv, w_q, scale):
    b, s, d = x.shape
    tm = ROW_TILE
    nat = jax.ShapeDtypeStruct((b, s, d), BF16)
    maj = jax.ShapeDtypeStruct((b, DMAJ, s // DMAJ, d), BF16)
    maj_spec = pl.BlockSpec((1, DMAJ, tm // DMAJ, d), lambda bb, i: (bb, 0, i, 0))
    return pl.pallas_call(
        functools.partial(_kvq_kernel, scale=scale),
        grid=(b, s // tm),
        in_specs=[_row_spec(tm, d), _resident((1, d)), _resident((1, d)),
                  _resident((d, 2 * d)), _resident((d, d))],
        out_specs=[_row_spec(tm, d)] * 4 + [maj_spec] * 4,
        out_shape=[nat] * 4 + [maj] * 4,
        scratch_shapes=[pltpu.VMEM((8, tm, LANES), F32)] * 2,
        compiler_params=_DENSE_PARAMS,
        name="kvq",
    )(x, kv_gain.reshape(1, d), q_gain.reshape(1, d), w_kv.astype(BF16), w_q.astype(BF16))


def _t5_bucket(dist):
    max_exact = REL_BUCKETS // 2
    n = jnp.maximum(dist, 0)
    nf = jnp.maximum(n, max_exact).astype(jnp.float32)
    large = max_exact + (jnp.log(nf / max_exact) / math.log(REL_MAX_DISTANCE / max_exact)
                         * (REL_BUCKETS - max_exact)).astype(jnp.int32)
    large = jnp.minimum(large, REL_BUCKETS - 1)
    return jnp.where(n < max_exact, n, large)


def _bias_table(rel_bias, qpos, kpos, dilation, window):
    delta = qpos[:, None] - kpos[None, :]
    band = (delta >= 0) & (delta <= window // dilation)
    onehot = jax.nn.one_hot(_t5_bucket(delta * dilation), REL_BUCKETS, dtype=F32)
    bias = jnp.einsum("qkn,nh->hqk", onehot, rel_bias.astype(F32), precision=lax.Precision.HIGHEST)
    return jnp.where(band[None], bias, MASKED)


def _bias_tables(rel_bias):
    p = BLOCK
    (w1, d1), (w4, d4), (w16, d16) = DILATED_BRANCHES
    runs = DMAJ // d4
    ar = lambda n: jnp.arange(n, dtype=jnp.int32)
    q_perm = (ar(p) % (p // runs)) * runs + ar(p) // (p // runs)
    k_perm_wide = (ar(2 * p) % (2 * p // runs)) * runs + ar(2 * p) // (2 * p // runs) - p
    wide = [
        _bias_table(rel_bias, ar(p), ar(2 * p) - p, d1, w1),
        _bias_table(rel_bias, ar(p), ar(2 * p), d1, w1),
        _bias_table(rel_bias, q_perm, k_perm_wide, d4, w4),
        _bias_table(rel_bias, ar(p), ar(2 * p) - p, d16, w16),
    ]
    narrow = [
        _bias_table(rel_bias, q_perm, q_perm, d4, w4),
        _bias_table(rel_bias, ar(p), ar(p), d16, w16),
    ]

    def pack(tabs):
        t = jnp.stack(tabs, axis=0)
        t = t.reshape(t.shape[0], N_HEADS // 2, 2, p, t.shape[-1])
        return t.transpose(1, 0, 2, 3, 4)

    return pack(wide), pack(narrow)


D1_INNER, D1_FIRST, D4_INNER, D16_INNER = range(4)
D4_FIRST, D16_FIRST = range(2)
ATTN_GROUP = 8


def _attn_kernel(qn_ref, kn_ref, van_ref, vbn_ref, qm_ref, km_ref, vam_ref, vbm_ref, wide_ref, narrow_ref,
                 o_ref, acc_ref, den_ref, mx_ref, mxs_ref, out_ref):
    s = qn_ref.shape[1]
    p = BLOCK
    (_, d1), (_, d4), (_, d16) = DILATED_BRANCHES
    runs = DMAJ // d4
    rl = p // runs
    quarter = s // d4
    head0 = lax.broadcasted_iota(jnp.int32, (1, LANES), 1) < LANES // 2

    def run_group(blocks):
        scores = []
        for q, k, _, _, tab_ref, kind, _ in blocks:
            qb, kw = q(), k()
            for hh in range(2):
                qh = jnp.where(head0 if hh == 0 else jnp.logical_not(head0), qb, jnp.zeros_like(qb))
                sc = lax.dot_general(qh, kw, (((1,), (1,)), ((), ())), preferred_element_type=F32)
                scores.append(sc + tab_ref[0, kind, hh])
        probs, maxes = [], []
        for sc in scores:
            m = jnp.max(sc, axis=-1, keepdims=True)
            probs.append(jnp.exp(sc - m).astype(BF16))
            maxes.append(jnp.broadcast_to(m, (p, LANES)))
        for i, (_, _, va, vb, _, _, store_fn) in enumerate(blocks):
            o0, o1 = _dot(probs[2 * i], va()), _dot(probs[2 * i + 1], vb())
            m0, m1 = maxes[2 * i], maxes[2 * i + 1]
            store_fn((jnp.where(head0, o0, o1), jnp.where(head0, o1, o0),
                      jnp.where(head0, m0, m1), jnp.where(head0, m1, m0)))

    def store(bi, rows, lo, res):
        for ref, val in zip((acc_ref, den_ref, mx_ref, mxs_ref), res):
            ref[bi, rows, :] = val[lo:lo + rows.size]

    def d1_body(i, carry):
        blocks = []
        for u in range(ATTN_GROUP):
            g = i * ATTN_GROUP + u
            row0 = pl.multiple_of(g * p, p)
            win = pl.ds(pl.multiple_of(jnp.maximum(g - 1, 0) * p, p), 2 * p)
            blocks.append((
                lambda row0=row0: qn_ref[0, pl.ds(row0, p), :],
                lambda win=win: kn_ref[0, win, :],
                lambda win=win: van_ref[0, win, :],
                lambda win=win: vbn_ref[0, win, :],
                wide_ref, jnp.where(g == 0, D1_FIRST, D1_INNER),
                lambda res, row0=row0: store(0, pl.ds(row0, p), 0, res)))
        run_group(blocks)
        return carry

    lax.fori_loop(0, s // p // ATTN_GROUP, d1_body, 0)

    nb16 = s // d16 // p

    def d16_body(i, carry):
        blocks = []
        for u in range(ATTN_GROUP // nb16):
            r = i * (ATTN_GROUP // nb16) + u
            base = (r % d4) * quarter + r // d4
            for j in range(nb16):
                win = slice(max(j - 1, 0) * p, (j + 1) * p)
                rows = pl.ds(base + j * p * (d16 // d4), p, stride=d16 // d4)
                blocks.append((
                    lambda r=r, j=j: qm_ref[0, r, j * p:(j + 1) * p, :],
                    lambda r=r, win=win: km_ref[0, r, win, :],
                    lambda r=r, win=win: vam_ref[0, r, win, :],
                    lambda r=r, win=win: vbm_ref[0, r, win, :],
                    narrow_ref if j == 0 else wide_ref, D16_FIRST if j == 0 else D16_INNER,
                    lambda res, rows=rows: store(2, rows, 0, res)))
        run_group(blocks)
        return carry

    lax.fori_loop(0, d16 * nb16 // ATTN_GROUP, d16_body, 0)

    nb4 = s // d4 // p

    def d4_body(r4, carry):
        def gather(ref, lo, n):
            return jnp.concatenate([ref[0, c * d4 + r4, lo:lo + n, :] for c in range(runs)], axis=0)

        def store_runs(res, j):
            for c in range(runs):
                store(1, pl.ds(r4 * quarter + j * p + c, rl, stride=runs), c * rl, res)

        blocks = []
        for j in range(nb4):
            lo = max(j - 1, 0) * rl
            width = rl if j == 0 else 2 * rl
            blocks.append((
                lambda j=j: gather(qm_ref, j * rl, rl),
                lambda lo=lo, width=width: gather(km_ref, lo, width),
                lambda lo=lo, width=width: gather(vam_ref, lo, width),
                lambda lo=lo, width=width: gather(vbm_ref, lo, width),
                narrow_ref if j == 0 else wide_ref, D4_FIRST if j == 0 else D4_INNER,
                lambda res, j=j: store_runs(res, j)))
        for g0 in range(0, nb4, ATTN_GROUP):
            run_group(blocks[g0:g0 + ATTN_GROUP])
        return carry

    lax.fori_loop(0, d4, d4_body, 0)

    mt = 2 * p
    per_quarter = quarter // mt

    def merge_body(i, carry):
        r4 = i // per_quarter
        a = (i % per_quarter) * mt
        rows_maj = pl.ds(pl.multiple_of(r4 * quarter + a, mt), mt)
        rows_tok = pl.ds(a * d4 + r4, mt, stride=d4)
        rows = (rows_tok, rows_maj, rows_maj)
        ms = [mx_ref[bi, rows[bi], :] for bi in range(3)]
        mss = [mxs_ref[bi, rows[bi], :] for bi in range(3)]
        m_all = functools.reduce(jnp.maximum, ms)
        ms_all = functools.reduce(jnp.maximum, mss)
        num = jnp.zeros((mt, LANES), F32)
        den_swapped = jnp.zeros((mt, LANES), F32)
        for bi in range(3):
            num = num + jnp.exp(ms[bi] - m_all) * acc_ref[bi, rows[bi], :]
            den_swapped = den_swapped + jnp.exp(mss[bi] - ms_all) * den_ref[bi, rows[bi], :]
        out_ref[rows_tok, :] = num / pltpu.roll(den_swapped, LANES // 2, 1)
        return carry

    lax.fori_loop(0, s // mt, merge_body, 0)

    def cast_body(i, carry):
        rows = pl.ds(pl.multiple_of(i * mt, mt), mt)
        o_ref[0, rows, :] = out_ref[rows, :].astype(o_ref.dtype)
        return carry

    lax.fori_loop(0, s // mt, cast_body, 0)


def _attention(nat, maj, wide, narrow):
    b, s, d = nat[0].shape
    nat_spec = pl.BlockSpec((1, s, LANES), lambda hp, bb: (bb, 0, hp))
    maj_spec = pl.BlockSpec((1, DMAJ, s // DMAJ, LANES), lambda hp, bb: (bb, 0, 0, hp))
    tab_spec = lambda t: pl.BlockSpec((1,) + t.shape[1:], lambda hp, bb: (hp, 0, 0, 0, 0))
    per_branch = pltpu.VMEM((len(DILATED_BRANCHES), s, LANES), F32)
    return pl.pallas_call(
        _attn_kernel,
        grid=(d // LANES, b),
        in_specs=[nat_spec] * 4 + [maj_spec] * 4 + [tab_spec(wide), tab_spec(narrow)],
        out_specs=nat_spec,
        out_shape=jax.ShapeDtypeStruct((b, s, d), BF16),
        scratch_shapes=[per_branch] * 4 + [pltpu.VMEM((s, LANES), F32)],
        compiler_params=pltpu.CompilerParams(
            dimension_semantics=("arbitrary", "arbitrary"), vmem_limit_bytes=VMEM_LIMIT),
        name="dilated_attn",
    )(*nat, *maj, wide, narrow)


def kernel(x, a_norm, a_w_in, a_conv, a_w_out, kv_norm, w_kv, b_norm, b_w_q, b_w_o, rel_bias,
           ffn_norm, ffn_w_up, ffn_conv, ffn_conv_b, ffn_w_down, final_norm):
    d = x.shape[-1]
    n_a, n_b = a_norm.shape[0], b_norm.shape[0]
    depth = n_a + n_b
    assert n_b == 1, "the shared K/V and the query projection are fused for a single attention layer"
    scale = (d // N_HEADS) ** -0.5
    wide, narrow = _bias_tables(rel_bias)

    h = x
    for l in range(n_a):
        h = _mixer(h, a_norm[l], a_w_in[l], a_conv[l], a_w_out[l])
        h = _ffn(h, ffn_norm[l], ffn_w_up[l], ffn_conv[l], ffn_conv_b[l], ffn_w_down[l])
    outs = _kvq(h, kv_norm, b_norm[0], w_kv, b_w_q[0], scale)
    att = _attention(outs[:4], outs[4:], wide, narrow)
    l = depth - 1
    return _ffn(h, ffn_norm[l], ffn_w_up[l], ffn_conv[l], ffn_conv_b[l], ffn_w_down[l],
                att=att, w_o=b_w_o[0], final_gain=final_norm)
```

```python
import functools
import math

import jax
import jax.numpy as jnp
from jax import lax
from jax.experimental import pallas as pl
from jax.experimental.pallas import tpu as pltpu

N_HEADS = 16
CONV_WIDTH = 3
DILATED_BRANCHES = ((128, 1), (512, 4), (2048, 16))
BLOCK = 128
REL_BUCKETS = 32
REL_MAX_DISTANCE = 2048
RMS_EPS = 1e-6

LANES = 128
SUBLANES = 8
ROW_TILE = 512
COL_CHUNK = 256
VMEM_LIMIT = 56 * 1024 * 1024
MASKED = -1e30
LOG2E = math.log2(math.e)

BF16 = jnp.bfloat16
F32 = jnp.float32


def _dot(a, b):
    return jnp.dot(a, b, preferred_element_type=F32)


def _rms_scale(x):
    return x * lax.rsqrt(jnp.mean(x * x, axis=-1, keepdims=True) + RMS_EPS)


def _shift_rows(cur, prev, k):
    rolled = pltpu.roll(cur, k, 0)
    top = pltpu.roll(prev, k, 0)
    row = lax.broadcasted_iota(jnp.int32, prev.shape, 0)
    first = jnp.where(row < k, top, rolled[:SUBLANES])
    return jnp.concatenate([first, rolled[SUBLANES:]], axis=0)


def _causal_conv3(u, carry_ref, w_ref, cols):
    prev = carry_ref[:, cols]
    carry_ref[:, cols] = u[u.shape[0] - SUBLANES:]
    u1 = _shift_rows(u, prev, 1)
    u2 = _shift_rows(u, prev, 2)
    return w_ref[0:1, cols] * u2 + w_ref[1:2, cols] * u1 + w_ref[2:3, cols] * u


def _resident(shape):
    return pl.BlockSpec(shape, lambda b, i: (0,) * len(shape), pipeline_mode=pl.Buffered(1))


def _row_spec(tm, width):
    return pl.BlockSpec((1, tm, width), lambda b, i: (b, i, 0))


_DENSE_PARAMS = pltpu.CompilerParams(
    dimension_semantics=("arbitrary", "arbitrary"), vmem_limit_bytes=VMEM_LIMIT)


def _mixer_kernel(x_ref, g_ref, win_ref, cw_ref, wout_ref, o_ref, carry_ref, y_ref):
    d = x_ref.shape[2]
    x = x_ref[0]
    xn = (_rms_scale(x) * g_ref[...]).astype(BF16)

    @pl.when(pl.program_id(1) == 0)
    def _():
        carry_ref[...] = jnp.zeros_like(carry_ref)

    for j in range(d // COL_CHUNK):
        cols = slice(j * COL_CHUNK, (j + 1) * COL_CHUNK)
        gate_b = _dot(xn, win_ref[:, cols])
        gate_c = _dot(xn, win_ref[:, d + j * COL_CHUNK:d + (j + 1) * COL_CHUNK])
        hh = _dot(xn, win_ref[:, 2 * d + j * COL_CHUNK:2 * d + (j + 1) * COL_CHUNK])
        ch = gate_c * hh
        conv = _causal_conv3(ch, carry_ref, cw_ref, cols)
        y_ref[:, cols] = (gate_b * conv).astype(BF16)
    o_ref[0] = x + _dot(y_ref[...], wout_ref[...])


def _mixer(x, gain, w_in, conv_w, w_out):
    b, s, d = x.shape
    tm = ROW_TILE
    return pl.pallas_call(
        _mixer_kernel,
        grid=(b, s // tm),
        in_specs=[_row_spec(tm, d), _resident((1, d)), _resident((d, 3 * d)),
                  _resident((CONV_WIDTH, d)), _resident((d, d))],
        out_specs=_row_spec(tm, d),
        out_shape=jax.ShapeDtypeStruct((b, s, d), F32),
        scratch_shapes=[pltpu.VMEM((SUBLANES, d), F32), pltpu.VMEM((tm, d), BF16)],
        compiler_params=_DENSE_PARAMS,
        name="mixer",
    )(x, gain.reshape(1, d), w_in.astype(BF16), conv_w, w_out.astype(BF16))


def _ffn_kernel(*refs, has_attn, has_final):
    refs = list(refs)
    x_ref = refs.pop(0)
    att_ref, wo_ref = (refs.pop(0), refs.pop(0)) if has_attn else (None, None)
    g_ref, wup_ref, cw_ref, cb_ref, wdn_ref = (refs.pop(0) for _ in range(5))
    fg_ref = refs.pop(0) if has_final else None
    o_ref, carry_ref, act_ref = refs
    f = wdn_ref.shape[0]

    x = x_ref[0]
    if has_attn:
        x = x + _dot(att_ref[0], wo_ref[...])
    xn = (_rms_scale(x) * g_ref[...]).astype(BF16)

    @pl.when(pl.program_id(1) == 0)
    def _():
        carry_ref[...] = jnp.zeros_like(carry_ref)

    def conv_chunk(c0):
        cols = slice(c0, c0 + COL_CHUNK)
        u = _dot(xn, wup_ref[:, cols])
        return _causal_conv3(u, carry_ref, cw_ref, cols) + cb_ref[:, cols]

    for j in range(f // COL_CHUNK):
        gate = conv_chunk(j * COL_CHUNK)
        up = conv_chunk(f + j * COL_CHUNK)
        half = 0.5 * gate
        act = half * (1.0 + jnp.tanh(half)) * up
        act_ref[:, j * COL_CHUNK:(j + 1) * COL_CHUNK] = act.astype(BF16)
    y = x + _dot(act_ref[...], wdn_ref[...])
    if has_final:
        y = _rms_scale(y) * fg_ref[...]
    o_ref[0] = y


def _ffn(x, gain, w_up, conv_w, conv_b, w_down, att=None, w_o=None, final_gain=None):
    b, s, d = x.shape
    f = w_down.shape[0]
    tm = ROW_TILE
    has_attn, has_final = att is not None, final_gain is not None
    args, specs = [x], [_row_spec(tm, d)]
    if has_attn:
        args += [att, w_o.astype(BF16)]
        specs += [_row_spec(tm, d), _resident((d, d))]
    args += [gain.reshape(1, d), w_up.astype(BF16), conv_w, conv_b.reshape(1, 2 * f), w_down.astype(BF16)]
    specs += [_resident((1, d)), _resident((d, 2 * f)), _resident((CONV_WIDTH, 2 * f)),
              _resident((1, 2 * f)), _resident((f, d))]
    if has_final:
        args.append(final_gain.reshape(1, d))
        specs.append(_resident((1, d)))
    return pl.pallas_call(
        functools.partial(_ffn_kernel, has_attn=has_attn, has_final=has_final),
        grid=(b, s // tm),
        in_specs=specs,
        out_specs=_row_spec(tm, d),
        out_shape=jax.ShapeDtypeStruct((b, s, d), F32),
        scratch_shapes=[pltpu.VMEM((SUBLANES, 2 * f), F32), pltpu.VMEM((tm, f), BF16)],
        compiler_params=_DENSE_PARAMS,
        name="ffn_attn" if has_attn else "ffn",
    )(*args)


DMAJ = 16


def _kvq_kernel(x_ref, gkv_ref, gq_ref, wkv_ref, wq_ref, *refs, scale):
    (q_nat, k_nat, v_nat), (q_maj, k_maj, v_maj) = refs[0:3], refs[3:6]
    tok_ref, mid_ref = refs[6:8]
    tm, d = x_ref.shape[1], x_ref.shape[2]
    step = 4
    y = _rms_scale(x_ref[0])
    xkv = (y * gkv_ref[...]).astype(BF16)
    xq = (y * gq_ref[...]).astype(BF16)
    head0 = lax.broadcasted_iota(jnp.int32, (1, LANES), 1) < LANES // 2
    slab = [0]

    def emit(val, lanes, nat_ref, maj_ref):
        nat_ref[0, :, lanes] = val.astype(BF16)
        i = slab[0] = (slab[0] + 1) % tok_ref.shape[0]
        tok_ref[i] = val
        for r in range(step):
            mid_ref[i, r * (tm // step):(r + 1) * (tm // step), :] = tok_ref[i, pl.ds(r, tm // step, stride=step), :]
        for r in range(DMAJ):
            rows = pl.ds((r % step) * (tm // step) + r // step, tm // DMAJ, stride=step)
            maj_ref[0, r, :, lanes] = mid_ref[i, rows, :].astype(BF16)

    for j in range(d // COL_CHUNK):
        qc = _dot(xq, wq_ref[:, j * COL_CHUNK:(j + 1) * COL_CHUNK]) * scale
        kc = _dot(xkv, wkv_ref[:, j * COL_CHUNK:(j + 1) * COL_CHUNK])
        vc = _dot(xkv, wkv_ref[:, d + j * COL_CHUNK:d + (j + 1) * COL_CHUNK])
        for h in range(COL_CHUNK // LANES):
            lanes = slice(j * COL_CHUNK + h * LANES, j * COL_CHUNK + (h + 1) * LANES)
            part = slice(h * LANES, (h + 1) * LANES)
            emit(qc[:, part], lanes, q_nat, q_maj)
            emit(kc[:, part], lanes, k_nat, k_maj)
            emit(vc[:, part], lanes, v_nat, v_maj)


def _kvq(x, kv_gain, q_gain, w_kv, w_q, scale):
    b, s, d = x.shape
    tm = ROW_TILE
    nat = jax.ShapeDtypeStruct((b, s, d), BF16)
    maj = jax.ShapeDtypeStruct((b, DMAJ, s // DMAJ, d), BF16)
    maj_spec = pl.BlockSpec((1, DMAJ, tm // DMAJ, d), lambda bb, i: (bb, 0, i, 0))
    return pl.pallas_call(
        functools.partial(_kvq_kernel, scale=scale),
        grid=(b, s // tm),
        in_specs=[_row_spec(tm, d), _resident((1, d)), _resident((1, d)),
                  _resident((d, 2 * d)), _resident((d, d))],
        out_specs=[_row_spec(tm, d)] * 3 + [maj_spec] * 3,
        out_shape=[nat] * 3 + [maj] * 3,
        scratch_shapes=[pltpu.VMEM((8, tm, LANES), F32)] * 2,
        compiler_params=_DENSE_PARAMS,
        name="kvq",
    )(x, kv_gain.reshape(1, d), q_gain.reshape(1, d), w_kv.astype(BF16), w_q.astype(BF16))


def _t5_bucket(dist):
    max_exact = REL_BUCKETS // 2
    n = jnp.maximum(dist, 0)
    nf = jnp.maximum(n, max_exact).astype(jnp.float32)
    large = max_exact + (jnp.log(nf / max_exact) / math.log(REL_MAX_DISTANCE / max_exact)
                         * (REL_BUCKETS - max_exact)).astype(jnp.int32)
    large = jnp.minimum(large, REL_BUCKETS - 1)
    return jnp.where(n < max_exact, n, large)


def _bias_table(rel_bias, qpos, kpos, dilation, window):
    delta = qpos[:, None] - kpos[None, :]
    band = (delta >= 0) & (delta <= window // dilation)
    onehot = jax.nn.one_hot(_t5_bucket(delta * dilation), REL_BUCKETS, dtype=F32)
    bias = jnp.einsum("qkn,nh->hqk", onehot, rel_bias.astype(F32), precision=lax.Precision.HIGHEST)
    return jnp.where(band[None], bias * LOG2E, MASKED)


def _bias_tables(rel_bias):
    p = BLOCK
    (w1, d1), (w4, d4), (w16, d16) = DILATED_BRANCHES
    runs = DMAJ // d4
    ar = lambda n: jnp.arange(n, dtype=jnp.int32)
    q_perm = (ar(p) % (p // runs)) * runs + ar(p) // (p // runs)
    k_perm_wide = (ar(2 * p) % (2 * p // runs)) * runs + ar(2 * p) // (2 * p // runs) - p
    wide = [
        _bias_table(rel_bias, ar(p), ar(2 * p) - p, d1, w1),
        _bias_table(rel_bias, ar(p), ar(2 * p), d1, w1),
        _bias_table(rel_bias, q_perm, k_perm_wide, d4, w4),
        _bias_table(rel_bias, ar(p), ar(2 * p) - p, d16, w16),
    ]
    narrow = [
        _bias_table(rel_bias, q_perm, q_perm, d4, w4),
        _bias_table(rel_bias, ar(p), ar(p), d16, w16),
    ]

    def pack(tabs):
        t = jnp.stack(tabs, axis=0)
        t = t.reshape(t.shape[0], N_HEADS // 2, 2, p, t.shape[-1])
        return t.transpose(1, 0, 2, 3, 4)

    return pack(wide), pack(narrow)


D1_INNER, D1_FIRST, D4_INNER, D16_INNER = range(4)
D4_FIRST, D16_FIRST = range(2)
ATTN_GROUP = 4


def _attn_kernel(zero_ref, qn_ref, kn_ref, vn_ref, qm_ref, km_ref, vm_ref, wide_ref, narrow_ref,
                 o_ref, acc_ref, den_ref, mx_ref, out_ref, s_ring, p_ring):
    s = qn_ref.shape[1]
    p = BLOCK
    (_, d1), (_, d4), (_, d16) = DILATED_BRANCHES
    runs = DMAJ // d4
    rl = p // runs
    quarter = s // d4
    head0 = lax.broadcasted_iota(jnp.int32, (1, LANES), 1) < LANES // 2
    one = jnp.ones((), BF16)

    zero = zero_ref[0]

    def ring(slot, u):
        return slot * 2 * ATTN_GROUP + u + zero

    def score_stage(blocks, slot):
        for i, (q, k, _, tab_ref, kind, _, _) in enumerate(blocks):
            qb, kw = q(), k()
            for hh in range(2):
                qh = jnp.where(head0 if hh == 0 else jnp.logical_not(head0), qb, jnp.zeros_like(qb))
                sc = lax.dot_general(qh, kw, (((1,), (1,)), ((), ())), preferred_element_type=F32)
                s_ring[ring(slot, 2 * i + hh), :, 0:kw.shape[0]] = sc + tab_ref[0, kind, hh]

    def softmax_stage(blocks, slot):
        for i, (_, k, _, _, _, put, _) in enumerate(blocks):
            maxes = []
            for u in (2 * i, 2 * i + 1):
                sc = s_ring[ring(slot, u), :, 0:k.width]
                m = jnp.max(sc, axis=-1, keepdims=True)
                p_ring[ring(slot, u), :, 0:k.width] = jnp.exp2(sc - m).astype(BF16)
                maxes.append(jnp.broadcast_to(m, (p, LANES)))
            put(mx_ref, jnp.where(head0, maxes[0], maxes[1]))

    def value_stage(blocks, slot):
        for i, (_, k, v, _, _, put, after) in enumerate(blocks):
            vw = v()
            o0 = _dot(p_ring[ring(slot, 2 * i), :, 0:k.width], jnp.where(head0, vw, one))
            o1 = _dot(p_ring[ring(slot, 2 * i + 1), :, 0:k.width], jnp.where(head0, one, vw))
            put(acc_ref, jnp.where(head0, o0, o1))
            put(den_ref, pltpu.roll(jnp.where(head0, o1, o0), LANES // 2, 1))
            if after is not None:
                after()

    def window(fn, width):
        fn.width = width
        return fn

    def put_rows(bi, rows):
        def put(ref, val):
            ref[bi, rows, :] = val
        return put

    blocks1, blocks16, blocks4 = [], [], {}
    for g in range(s // p):
        row0, win = g * p, slice(max(g - 1, 0) * p, max(g - 1, 0) * p + 2 * p)
        blocks1.append((
            lambda row0=row0: qn_ref[0, row0:row0 + p, :],
            window(lambda win=win: kn_ref[0, win, :], 2 * p),
            lambda win=win: vn_ref[0, win, :],
            wide_ref, D1_FIRST if g == 0 else D1_INNER, put_rows(0, pl.ds(row0, p)), None))
    for r in range(d16):
        base = (r % d4) * quarter + r // d4
        for j in range(s // d16 // p):
            win = slice(max(j - 1, 0) * p, (j + 1) * p)
            blocks16.append((
                lambda r=r, j=j: qm_ref[0, r, j * p:(j + 1) * p, :],
                window(lambda r=r, win=win: km_ref[0, r, win, :], win.stop - win.start),
                lambda r=r, win=win: vm_ref[0, r, win, :],
                narrow_ref if j == 0 else wide_ref, D16_FIRST if j == 0 else D16_INNER,
                put_rows(2, pl.ds(base + j * p * (d16 // d4), p, stride=d16 // d4)), None))
    for r4 in range(d4):
        def gather(ref, lo, n, r4=r4):
            return jnp.concatenate([ref[0, c * d4 + r4, lo:lo + n, :] for c in range(runs)], axis=0)

        def put_runs(j, r4=r4):
            def put(ref, val):
                for c in range(runs):
                    ref[1, pl.ds(r4 * quarter + j * p + c, rl, stride=runs), :] = val[c * rl:(c + 1) * rl]
            return put

        for j in range(s // d4 // p):
            lo = max(j - 1, 0) * rl
            width = rl if j == 0 else 2 * rl
            blocks4[r4, j] = (
                lambda j=j, gather=gather: gather(qm_ref, j * rl, rl),
                window(lambda lo=lo, width=width, gather=gather: gather(km_ref, lo, width), runs * width),
                lambda lo=lo, width=width, gather=gather: gather(vm_ref, lo, width),
                narrow_ref if j == 0 else wide_ref, D4_FIRST if j == 0 else D4_INNER, put_runs(j), None)

    blocks = blocks1 + blocks16 + [blocks4[r4, j] for r4 in range(d4) for j in range(s // d4 // p)]

    pending = None
    for n, g0 in enumerate(range(0, len(blocks), ATTN_GROUP)):
        group = blocks[g0:g0 + ATTN_GROUP]
        score_stage(group, n % 2)
        if pending is not None:
            value_stage(*pending)
        softmax_stage(group, n % 2)
        pending = (group, n % 2)
    value_stage(*pending)

    mt = 2 * p
    per_quarter = quarter // mt

    def merge_body(i, carry):
        r4 = i // per_quarter
        a = (i % per_quarter) * mt
        rows_maj = pl.ds(pl.multiple_of(r4 * quarter + a, mt), mt)
        rows_tok = pl.ds(a * d4 + r4, mt, stride=d4)
        rows = (rows_tok, rows_maj, rows_maj)
        ms = [mx_ref[bi, rows[bi], :] for bi in range(3)]
        m_all = functools.reduce(jnp.maximum, ms)
        num = jnp.zeros((mt, LANES), F32)
        den = jnp.zeros((mt, LANES), F32)
        for bi in range(3):
            w = jnp.exp2(ms[bi] - m_all)
            num = num + w * acc_ref[bi, rows[bi], :]
            den = den + w * den_ref[bi, rows[bi], :]
        out_ref[rows_tok, :] = num / den
        return carry

    lax.fori_loop(0, s // mt, merge_body, 0)

    def cast_body(i, carry):
        rows = pl.ds(pl.multiple_of(i * mt, mt), mt)
        o_ref[0, rows, :] = out_ref[rows, :].astype(o_ref.dtype)
        return carry

    lax.fori_loop(0, s // mt, cast_body, 0)


def _attention(nat, maj, wide, narrow):
    b, s, d = nat[0].shape
    nat_spec = pl.BlockSpec((1, s, LANES), lambda hp, bb: (bb, 0, hp))
    maj_spec = pl.BlockSpec((1, DMAJ, s // DMAJ, LANES), lambda hp, bb: (bb, 0, 0, hp))
    tab_spec = lambda t: pl.BlockSpec((1,) + t.shape[1:], lambda hp, bb: (hp, 0, 0, 0, 0))
    per_branch = pltpu.VMEM((len(DILATED_BRANCHES), s, LANES), F32)
    return pl.pallas_call(
        _attn_kernel,
        grid=(d // LANES, b),
        in_specs=[pl.BlockSpec(memory_space=pltpu.SMEM)] + [nat_spec] * 3 + [maj_spec] * 3
        + [tab_spec(wide), tab_spec(narrow)],
        out_specs=nat_spec,
        out_shape=jax.ShapeDtypeStruct((b, s, d), BF16),
        scratch_shapes=[per_branch] * 3 + [
            pltpu.VMEM((s, LANES), F32),
            pltpu.VMEM((2 * 2 * ATTN_GROUP, BLOCK, 2 * BLOCK), F32),
            pltpu.VMEM((2 * 2 * ATTN_GROUP, BLOCK, 2 * BLOCK), BF16),
        ],
        compiler_params=pltpu.CompilerParams(
            dimension_semantics=("arbitrary", "arbitrary"), vmem_limit_bytes=VMEM_LIMIT),
        name="dilated_attn",
    )(jnp.zeros((1,), jnp.int32), *nat, *maj, wide, narrow)


def kernel(x, a_norm, a_w_in, a_conv, a_w_out, kv_norm, w_kv, b_norm, b_w_q, b_w_o, rel_bias,
           ffn_norm, ffn_w_up, ffn_conv, ffn_conv_b, ffn_w_down, final_norm):
    d = x.shape[-1]
    n_a, n_b = a_norm.shape[0], b_norm.shape[0]
    depth = n_a + n_b
    assert n_b == 1, "the shared K/V and the query projection are fused for a single attention layer"
    scale = (d // N_HEADS) ** -0.5 * LOG2E
    wide, narrow = _bias_tables(rel_bias)

    h = x
    for l in range(n_a):
        h = _mixer(h, a_norm[l], a_w_in[l], a_conv[l], a_w_out[l])
        h = _ffn(h, ffn_norm[l], ffn_w_up[l], ffn_conv[l], ffn_conv_b[l], ffn_w_down[l])
    outs = _kvq(h, kv_norm, b_norm[0], w_kv, b_w_q[0], scale)
    att = _attention(outs[:3], outs[3:], wide, narrow)
    l = depth - 1
    return _ffn(h, ffn_norm[l], ffn_w_up[l], ffn_conv[l], ffn_conv_b[l], ffn_w_down[l],
                att=att, w_o=b_w_o[0], final_gain=final_norm)
```

```python
import functools
import math

import jax
import jax.numpy as jnp
from jax import lax
from jax.experimental import pallas as pl
from jax.experimental.pallas import tpu as pltpu

N_HEADS = 16
CONV_WIDTH = 3
DILATED_BRANCHES = ((128, 1), (512, 4), (2048, 16))
BLOCK = 128
REL_BUCKETS = 32
REL_MAX_DISTANCE = 2048
RMS_EPS = 1e-6

LANES = 128
SUBLANES = 8
ROW_TILE = 512
COL_CHUNK = 256
VMEM_LIMIT = 56 * 1024 * 1024
MASKED = -1e30
LOG2E = math.log2(math.e)

BF16 = jnp.bfloat16
F32 = jnp.float32


def _dot(a, b):
    return jnp.dot(a, b, preferred_element_type=F32)


def _rms_scale(x):
    return x * lax.rsqrt(jnp.mean(x * x, axis=-1, keepdims=True) + RMS_EPS)


def _shift_rows(cur, prev, k):
    rolled = pltpu.roll(cur, k, 0)
    top = pltpu.roll(prev, k, 0)
    row = lax.broadcasted_iota(jnp.int32, prev.shape, 0)
    first = jnp.where(row < k, top, rolled[:SUBLANES])
    return jnp.concatenate([first, rolled[SUBLANES:]], axis=0)


def _causal_conv3(u, carry_ref, w_ref, cols):
    prev = carry_ref[:, cols]
    carry_ref[:, cols] = u[u.shape[0] - SUBLANES:]
    u1 = _shift_rows(u, prev, 1)
    u2 = _shift_rows(u, prev, 2)
    return w_ref[0:1, cols] * u2 + w_ref[1:2, cols] * u1 + w_ref[2:3, cols] * u


def _resident(shape):
    return pl.BlockSpec(shape, lambda b, i: (0,) * len(shape), pipeline_mode=pl.Buffered(1))


def _row_spec(tm, width):
    return pl.BlockSpec((1, tm, width), lambda b, i: (b, i, 0))


_DENSE_PARAMS = pltpu.CompilerParams(
    dimension_semantics=("arbitrary", "arbitrary"), vmem_limit_bytes=VMEM_LIMIT)


def _mixer_kernel(x_ref, g_ref, win_ref, cw_ref, wout_ref, o_ref, carry_ref, y_ref):
    d = x_ref.shape[2]
    x = x_ref[0]
    xn = (_rms_scale(x) * g_ref[...]).astype(BF16)

    @pl.when(pl.program_id(1) == 0)
    def _():
        carry_ref[...] = jnp.zeros_like(carry_ref)

    for j in range(d // COL_CHUNK):
        cols = slice(j * COL_CHUNK, (j + 1) * COL_CHUNK)
        gate_b = _dot(xn, win_ref[:, cols])
        gate_c = _dot(xn, win_ref[:, d + j * COL_CHUNK:d + (j + 1) * COL_CHUNK])
        hh = _dot(xn, win_ref[:, 2 * d + j * COL_CHUNK:2 * d + (j + 1) * COL_CHUNK])
        ch = gate_c * hh
        conv = _causal_conv3(ch, carry_ref, cw_ref, cols)
        y_ref[:, cols] = (gate_b * conv).astype(BF16)
    o_ref[0] = x + _dot(y_ref[...], wout_ref[...])


def _mixer(x, gain, w_in, conv_w, w_out):
    b, s, d = x.shape
    tm = ROW_TILE
    return pl.pallas_call(
        _mixer_kernel,
        grid=(b, s // tm),
        in_specs=[_row_spec(tm, d), _resident((1, d)), _resident((d, 3 * d)),
                  _resident((CONV_WIDTH, d)), _resident((d, d))],
        out_specs=_row_spec(tm, d),
        out_shape=jax.ShapeDtypeStruct((b, s, d), F32),
        scratch_shapes=[pltpu.VMEM((SUBLANES, d), F32), pltpu.VMEM((tm, d), BF16)],
        compiler_params=_DENSE_PARAMS,
        name="mixer",
    )(x, gain.reshape(1, d), w_in.astype(BF16), conv_w, w_out.astype(BF16))


def _ffn_kernel(*refs, has_attn, has_final):
    refs = list(refs)
    x_ref = refs.pop(0)
    att_ref, wo_ref = (refs.pop(0), refs.pop(0)) if has_attn else (None, None)
    g_ref, wup_ref, cw_ref, cb_ref, wdn_ref = (refs.pop(0) for _ in range(5))
    fg_ref = refs.pop(0) if has_final else None
    o_ref, carry_ref, act_ref = refs
    f = wdn_ref.shape[0]

    x = x_ref[0]
    if has_attn:
        att = jnp.concatenate([att_ref[0, hp] for hp in range(att_ref.shape[1])], axis=1)
        x = x + _dot(att, wo_ref[...])
    xn = (_rms_scale(x) * g_ref[...]).astype(BF16)

    @pl.when(pl.program_id(1) == 0)
    def _():
        carry_ref[...] = jnp.zeros_like(carry_ref)

    def conv_chunk(c0):
        cols = slice(c0, c0 + COL_CHUNK)
        u = _dot(xn, wup_ref[:, cols])
        return _causal_conv3(u, carry_ref, cw_ref, cols) + cb_ref[:, cols]

    for j in range(f // COL_CHUNK):
        gate = conv_chunk(j * COL_CHUNK)
        up = conv_chunk(f + j * COL_CHUNK)
        half = 0.5 * gate
        act = half * (1.0 + jnp.tanh(half)) * up
        act_ref[:, j * COL_CHUNK:(j + 1) * COL_CHUNK] = act.astype(BF16)
    y = x + _dot(act_ref[...], wdn_ref[...])
    if has_final:
        y = _rms_scale(y) * fg_ref[...]
    o_ref[0] = y


def _ffn(x, gain, w_up, conv_w, conv_b, w_down, att=None, w_o=None, final_gain=None):
    b, s, d = x.shape
    f = w_down.shape[0]
    tm = ROW_TILE
    has_attn, has_final = att is not None, final_gain is not None
    args, specs = [x], [_row_spec(tm, d)]
    if has_attn:
        args += [att, w_o.astype(BF16)]
        specs += [pl.BlockSpec((1, d // LANES, tm, LANES), lambda bb, i: (bb, 0, i, 0)), _resident((d, d))]
    args += [gain.reshape(1, d), w_up.astype(BF16), conv_w, conv_b.reshape(1, 2 * f), w_down.astype(BF16)]
    specs += [_resident((1, d)), _resident((d, 2 * f)), _resident((CONV_WIDTH, 2 * f)),
              _resident((1, 2 * f)), _resident((f, d))]
    if has_final:
        args.append(final_gain.reshape(1, d))
        specs.append(_resident((1, d)))
    return pl.pallas_call(
        functools.partial(_ffn_kernel, has_attn=has_attn, has_final=has_final),
        grid=(b, s // tm),
        in_specs=specs,
        out_specs=_row_spec(tm, d),
        out_shape=jax.ShapeDtypeStruct((b, s, d), F32),
        scratch_shapes=[pltpu.VMEM((SUBLANES, 2 * f), F32), pltpu.VMEM((tm, f), BF16)],
        compiler_params=_DENSE_PARAMS,
        name="ffn_attn" if has_attn else "ffn",
    )(*args)


DMAJ = 16


def _kvq_kernel(x_ref, gkv_ref, gq_ref, wkv_ref, wq_ref, *refs, scale):
    (q_nat, k_nat, v_nat), (q_maj, k_maj, v_maj) = refs[0:3], refs[3:6]
    tok_ref, mid_ref = refs[6:8]
    tm, d = x_ref.shape[1], x_ref.shape[2]
    step = 4
    y = _rms_scale(x_ref[0])
    xkv = (y * gkv_ref[...]).astype(BF16)
    xq = (y * gq_ref[...]).astype(BF16)
    head0 = lax.broadcasted_iota(jnp.int32, (1, LANES), 1) < LANES // 2
    slab = [0]

    def emit(val, pair, nat_ref, maj_ref):
        nat_ref[0, pair] = val.astype(BF16)
        i = slab[0] = (slab[0] + 1) % tok_ref.shape[0]
        tok_ref[i] = val
        for r in range(step):
            mid_ref[i, r * (tm // step):(r + 1) * (tm // step), :] = tok_ref[i, pl.ds(r, tm // step, stride=step), :]
        for r in range(DMAJ):
            rows = pl.ds((r % step) * (tm // step) + r // step, tm // DMAJ, stride=step)
            maj_ref[0, pair, r] = mid_ref[i, rows, :].astype(BF16)

    for j in range(d // COL_CHUNK):
        qc = _dot(xq, wq_ref[:, j * COL_CHUNK:(j + 1) * COL_CHUNK]) * scale
        kc = _dot(xkv, wkv_ref[:, j * COL_CHUNK:(j + 1) * COL_CHUNK])
        vc = _dot(xkv, wkv_ref[:, d + j * COL_CHUNK:d + (j + 1) * COL_CHUNK])
        for h in range(COL_CHUNK // LANES):
            pair = j * (COL_CHUNK // LANES) + h
            part = slice(h * LANES, (h + 1) * LANES)
            emit(qc[:, part], pair, q_nat, q_maj)
            emit(kc[:, part], pair, k_nat, k_maj)
            emit(vc[:, part], pair, v_nat, v_maj)


def _kvq(x, kv_gain, q_gain, w_kv, w_q, scale):
    b, s, d = x.shape
    tm = ROW_TILE
    pairs = d // LANES
    nat = jax.ShapeDtypeStruct((b, pairs, s, LANES), BF16)
    maj = jax.ShapeDtypeStruct((b, pairs, DMAJ, s // DMAJ, LANES), BF16)
    nat_spec = pl.BlockSpec((1, pairs, tm, LANES), lambda bb, i: (bb, 0, i, 0))
    maj_spec = pl.BlockSpec((1, pairs, DMAJ, tm // DMAJ, LANES), lambda bb, i: (bb, 0, 0, i, 0))
    return pl.pallas_call(
        functools.partial(_kvq_kernel, scale=scale),
        grid=(b, s // tm),
        in_specs=[_row_spec(tm, d), _resident((1, d)), _resident((1, d)),
                  _resident((d, 2 * d)), _resident((d, d))],
        out_specs=[nat_spec] * 3 + [maj_spec] * 3,
        out_shape=[nat] * 3 + [maj] * 3,
        scratch_shapes=[pltpu.VMEM((8, tm, LANES), F32)] * 2,
        compiler_params=_DENSE_PARAMS,
        name="kvq",
    )(x, kv_gain.reshape(1, d), q_gain.reshape(1, d), w_kv.astype(BF16), w_q.astype(BF16))


def _t5_bucket(dist):
    max_exact = REL_BUCKETS // 2
    n = jnp.maximum(dist, 0)
    nf = jnp.maximum(n, max_exact).astype(jnp.float32)
    large = max_exact + (jnp.log(nf / max_exact) / math.log(REL_MAX_DISTANCE / max_exact)
                         * (REL_BUCKETS - max_exact)).astype(jnp.int32)
    large = jnp.minimum(large, REL_BUCKETS - 1)
    return jnp.where(n < max_exact, n, large)


def _bias_table(rel_bias, qpos, kpos, dilation, window):
    delta = qpos[:, None] - kpos[None, :]
    band = (delta >= 0) & (delta <= window // dilation)
    onehot = jax.nn.one_hot(_t5_bucket(delta * dilation), REL_BUCKETS, dtype=F32)
    bias = jnp.einsum("qkn,nh->hqk", onehot, rel_bias.astype(F32), precision=lax.Precision.HIGHEST)
    return jnp.where(band[None], bias * LOG2E, MASKED)


def _bias_tables(rel_bias):
    p = BLOCK
    (w1, d1), (w4, d4), (w16, d16) = DILATED_BRANCHES
    runs = DMAJ // d4
    ar = lambda n: jnp.arange(n, dtype=jnp.int32)
    q_perm = (ar(p) % (p // runs)) * runs + ar(p) // (p // runs)
    k_perm_wide = (ar(2 * p) % (2 * p // runs)) * runs + ar(2 * p) // (2 * p // runs) - p
    wide = [
        _bias_table(rel_bias, ar(p), ar(2 * p) - p, d1, w1),
        _bias_table(rel_bias, ar(p), ar(2 * p), d1, w1),
        _bias_table(rel_bias, q_perm, k_perm_wide, d4, w4),
        _bias_table(rel_bias, ar(p), ar(2 * p) - p, d16, w16),
    ]
    narrow = [
        _bias_table(rel_bias, q_perm, q_perm, d4, w4),
        _bias_table(rel_bias, ar(p), ar(p), d16, w16),
    ]

    def pack(tabs):
        t = jnp.stack(tabs, axis=0)
        t = t.reshape(t.shape[0], N_HEADS // 2, 2, p, t.shape[-1])
        return t.transpose(1, 0, 2, 3, 4)

    return pack(wide), pack(narrow)


D1_INNER, D1_FIRST, D4_INNER, D16_INNER = range(4)
D4_FIRST, D16_FIRST = range(2)
ATTN_GROUP = 4


def _attn_kernel(zero_ref, qn_ref, kn_ref, vn_ref, qm_ref, km_ref, vm_ref, wide_ref, narrow_ref,
                 o_ref, acc_ref, den_ref, mx_ref, out_ref, s_ring, p_ring):
    s = qn_ref.shape[1]
    p = BLOCK
    (_, d1), (_, d4), (_, d16) = DILATED_BRANCHES
    runs = DMAJ // d4
    rl = p // runs
    quarter = s // d4
    head0 = lax.broadcasted_iota(jnp.int32, (1, LANES), 1) < LANES // 2
    one = jnp.ones((), BF16)

    zero = zero_ref[0]

    def ring(slot, u):
        return slot * 2 * ATTN_GROUP + u + zero

    def score_stage(blocks, slot):
        for i, (q, k, _, tab_ref, kind, _, _) in enumerate(blocks):
            qb, kw = q(), k()
            for hh in range(2):
                qh = jnp.where(head0 if hh == 0 else jnp.logical_not(head0), qb, jnp.zeros_like(qb))
                sc = lax.dot_general(qh, kw, (((1,), (1,)), ((), ())), preferred_element_type=F32)
                s_ring[ring(slot, 2 * i + hh), :, 0:kw.shape[0]] = sc + tab_ref[0, kind, hh]

    def softmax_stage(blocks, slot):
        for i, (_, k, _, _, _, put, _) in enumerate(blocks):
            maxes = []
            for u in (2 * i, 2 * i + 1):
                sc = s_ring[ring(slot, u), :, 0:k.width]
                m = jnp.max(sc, axis=-1, keepdims=True)
                p_ring[ring(slot, u), :, 0:k.width] = jnp.exp2(sc - m).astype(BF16)
                maxes.append(jnp.broadcast_to(m, (p, LANES)))
            put(mx_ref, jnp.where(head0, maxes[0], maxes[1]))

    def value_stage(blocks, slot):
        for i, (_, k, v, _, _, put, after) in enumerate(blocks):
            vw = v()
            o0 = _dot(p_ring[ring(slot, 2 * i), :, 0:k.width], jnp.where(head0, vw, one))
            o1 = _dot(p_ring[ring(slot, 2 * i + 1), :, 0:k.width], jnp.where(head0, one, vw))
            put(acc_ref, jnp.where(head0, o0, o1))
            put(den_ref, pltpu.roll(jnp.where(head0, o1, o0), LANES // 2, 1))
            if after is not None:
                after()

    def window(fn, width):
        fn.width = width
        return fn

    def put_rows(bi, rows):
        def put(ref, val):
            ref[bi, rows, :] = val
        return put

    blocks1, blocks16, blocks4 = [], [], {}
    for g in range(s // p):
        row0, win = g * p, slice(max(g - 1, 0) * p, max(g - 1, 0) * p + 2 * p)
        blocks1.append((
            lambda row0=row0: qn_ref[0, row0:row0 + p, :],
            window(lambda win=win: kn_ref[0, win, :], 2 * p),
            lambda win=win: vn_ref[0, win, :],
            wide_ref, D1_FIRST if g == 0 else D1_INNER, put_rows(0, pl.ds(row0, p)), None))
    for r in range(d16):
        base = (r % d4) * quarter + r // d4
        for j in range(s // d16 // p):
            win = slice(max(j - 1, 0) * p, (j + 1) * p)
            blocks16.append((
                lambda r=r, j=j: qm_ref[0, r, j * p:(j + 1) * p, :],
                window(lambda r=r, win=win: km_ref[0, r, win, :], win.stop - win.start),
                lambda r=r, win=win: vm_ref[0, r, win, :],
                narrow_ref if j == 0 else wide_ref, D16_FIRST if j == 0 else D16_INNER,
                put_rows(2, pl.ds(base + j * p * (d16 // d4), p, stride=d16 // d4)), None))
    for r4 in range(d4):
        def gather(ref, lo, n, r4=r4):
            return jnp.concatenate([ref[0, c * d4 + r4, lo:lo + n, :] for c in range(runs)], axis=0)

        def put_runs(j, r4=r4):
            def put(ref, val):
                for c in range(runs):
                    ref[1, pl.ds(r4 * quarter + j * p + c, rl, stride=runs), :] = val[c * rl:(c + 1) * rl]
            return put

        for j in range(s // d4 // p):
            lo = max(j - 1, 0) * rl
            width = rl if j == 0 else 2 * rl
            blocks4[r4, j] = (
                lambda j=j, gather=gather: gather(qm_ref, j * rl, rl),
                window(lambda lo=lo, width=width, gather=gather: gather(km_ref, lo, width), runs * width),
                lambda lo=lo, width=width, gather=gather: gather(vm_ref, lo, width),
                narrow_ref if j == 0 else wide_ref, D4_FIRST if j == 0 else D4_INNER, put_runs(j), None)

    blocks = blocks1 + blocks16 + [blocks4[r4, j] for r4 in range(d4) for j in range(s // d4 // p)]

    pending = None
    for n, g0 in enumerate(range(0, len(blocks), ATTN_GROUP)):
        group = blocks[g0:g0 + ATTN_GROUP]
        score_stage(group, n % 2)
        if pending is not None:
            value_stage(*pending)
        softmax_stage(group, n % 2)
        pending = (group, n % 2)
    value_stage(*pending)

    mt = 4 * p
    per_quarter = quarter // mt

    def merge_body(i, carry):
        r4 = i // per_quarter
        a = (i % per_quarter) * mt
        rows_maj = pl.ds(pl.multiple_of(r4 * quarter + a, mt), mt)
        rows_tok = pl.ds(a * d4 + r4, mt, stride=d4)
        rows = (rows_tok, rows_maj, rows_maj)
        ms = [mx_ref[bi, rows[bi], :] for bi in range(3)]
        m_all = functools.reduce(jnp.maximum, ms)
        num = jnp.zeros((mt, LANES), F32)
        den = jnp.zeros((mt, LANES), F32)
        for bi in range(3):
            w = jnp.exp2(ms[bi] - m_all)
            num = num + w * acc_ref[bi, rows[bi], :]
            den = den + w * den_ref[bi, rows[bi], :]
        out_ref[rows_tok, :] = num / den
        return carry

    lax.fori_loop(0, s // mt, merge_body, 0)

    def cast_body(i, carry):
        rows = pl.ds(pl.multiple_of(i * mt, mt), mt)
        o_ref[0, rows, :] = out_ref[rows, :].astype(o_ref.dtype)
        return carry

    lax.fori_loop(0, s // mt, cast_body, 0)


def _attention(nat, maj, wide, narrow):
    b, pairs, s, _ = nat[0].shape
    d = pairs * LANES
    nat_spec = pl.BlockSpec((1, None, s, LANES), lambda hp, bb: (bb, hp, 0, 0))
    maj_spec = pl.BlockSpec((1, None, DMAJ, s // DMAJ, LANES), lambda hp, bb: (bb, hp, 0, 0, 0))
    tab_spec = lambda t: pl.BlockSpec((1,) + t.shape[1:], lambda hp, bb: (hp, 0, 0, 0, 0))
    per_branch = pltpu.VMEM((len(DILATED_BRANCHES), s, LANES), F32)
    return pl.pallas_call(
        _attn_kernel,
        grid=(d // LANES, b),
        in_specs=[pl.BlockSpec(memory_space=pltpu.SMEM)] + [nat_spec] * 3 + [maj_spec] * 3
        + [tab_spec(wide), tab_spec(narrow)],
        out_specs=nat_spec,
        out_shape=jax.ShapeDtypeStruct((b, pairs, s, LANES), BF16),
        scratch_shapes=[per_branch] * 3 + [
            pltpu.VMEM((s, LANES), F32),
            pltpu.VMEM((2 * 2 * ATTN_GROUP, BLOCK, 2 * BLOCK), F32),
            pltpu.VMEM((2 * 2 * ATTN_GROUP, BLOCK, 2 * BLOCK), BF16),
        ],
        compiler_params=pltpu.CompilerParams(
            dimension_semantics=("arbitrary", "arbitrary"), vmem_limit_bytes=VMEM_LIMIT),
        name="dilated_attn",
    )(jnp.zeros((1,), jnp.int32), *nat, *maj, wide, narrow)


def kernel(x, a_norm, a_w_in, a_conv, a_w_out, kv_norm, w_kv, b_norm, b_w_q, b_w_o, rel_bias,
           ffn_norm, ffn_w_up, ffn_conv, ffn_conv_b, ffn_w_down, final_norm):
    d = x.shape[-1]
    n_a, n_b = a_norm.shape[0], b_norm.shape[0]
    depth = n_a + n_b
    assert n_b == 1, "the shared K/V and the query projection are fused for a single attention layer"
    scale = (d // N_HEADS) ** -0.5 * LOG2E
    wide, narrow = _bias_tables(rel_bias)

    h = x
    for l in range(n_a):
        h = _mixer(h, a_norm[l], a_w_in[l], a_conv[l], a_w_out[l])
        h = _ffn(h, ffn_norm[l], ffn_w_up[l], ffn_conv[l], ffn_conv_b[l], ffn_w_down[l])
    outs = _kvq(h, kv_norm, b_norm[0], w_kv, b_w_q[0], scale)
    att = _attention(outs[:3], outs[3:], wide, narrow)
    l = depth - 1
    return _ffn(h, ffn_norm[l], ffn_w_up[l], ffn_conv[l], ffn_conv_b[l], ffn_w_down[l],
                att=att, w_o=b_w_o[0], final_gain=final_norm)
```

```python
import functools
import math

import jax
import jax.numpy as jnp
from jax import lax
from jax.experimental import pallas as pl
from jax.experimental.pallas import tpu as pltpu

N_HEADS = 16
CONV_WIDTH = 3
DILATED_BRANCHES = ((128, 1), (512, 4), (2048, 16))
BLOCK = 128
REL_BUCKETS = 32
REL_MAX_DISTANCE = 2048
RMS_EPS = 1e-6

LANES = 128
SUBLANES = 8
ROW_TILE = 1024
COL_CHUNK = 256
VMEM_LIMIT = 56 * 1024 * 1024
MASKED = -1e30
LOG2E = math.log2(math.e)

BF16 = jnp.bfloat16
F32 = jnp.float32


def _dot(a, b):
    return jnp.dot(a, b, preferred_element_type=F32)


def _rms_scale(x):
    return x * lax.rsqrt(jnp.mean(x * x, axis=-1, keepdims=True) + RMS_EPS)


def _shift_rows(cur, prev, k):
    rolled = pltpu.roll(cur, k, 0)
    top = pltpu.roll(prev, k, 0)
    row = lax.broadcasted_iota(jnp.int32, prev.shape, 0)
    first = jnp.where(row < k, top, rolled[:SUBLANES])
    return jnp.concatenate([first, rolled[SUBLANES:]], axis=0)


def _causal_conv3(u, carry_ref, w_ref, cols):
    prev = carry_ref[:, cols]
    carry_ref[:, cols] = u[u.shape[0] - SUBLANES:]
    u1 = _shift_rows(u, prev, 1)
    u2 = _shift_rows(u, prev, 2)
    return w_ref[0:1, cols] * u2 + w_ref[1:2, cols] * u1 + w_ref[2:3, cols] * u


def _resident(shape):
    return pl.BlockSpec(shape, lambda b, i: (0,) * len(shape), pipeline_mode=pl.Buffered(1))


def _row_spec(tm, width):
    return pl.BlockSpec((1, tm, width), lambda b, i: (b, i, 0))


_DENSE_PARAMS = pltpu.CompilerParams(
    dimension_semantics=("arbitrary", "arbitrary"), vmem_limit_bytes=VMEM_LIMIT)


def _mixer_kernel(x_ref, g_ref, win_ref, cw_ref, wout_ref, o_ref, carry_ref, y_ref):
    d = x_ref.shape[2]

    @pl.when(pl.program_id(1) == 0)
    def _():
        carry_ref[...] = jnp.zeros_like(carry_ref)

    x = x_ref[0]
    xn = (_rms_scale(x) * g_ref[...]).astype(BF16)

    for j in range(d // COL_CHUNK):
        cols = slice(j * COL_CHUNK, (j + 1) * COL_CHUNK)
        gate_b = _dot(xn, win_ref[:, cols])
        gate_c = _dot(xn, win_ref[:, d + j * COL_CHUNK:d + (j + 1) * COL_CHUNK])
        hh = _dot(xn, win_ref[:, 2 * d + j * COL_CHUNK:2 * d + (j + 1) * COL_CHUNK])
        ch = gate_c * hh
        conv = _causal_conv3(ch, carry_ref, cw_ref, cols)
        y_ref[:, cols] = (gate_b * conv).astype(BF16)
    o_ref[0] = x + _dot(y_ref[...], wout_ref[...])


def _mixer(x, gain, w_in, conv_w, w_out):
    b, s, d = x.shape
    tm = ROW_TILE
    return pl.pallas_call(
        _mixer_kernel,
        grid=(b, s // tm),
        in_specs=[_row_spec(tm, d), _resident((1, d)), _resident((d, 3 * d)),
                  _resident((CONV_WIDTH, d)), _resident((d, d))],
        out_specs=_row_spec(tm, d),
        out_shape=jax.ShapeDtypeStruct((b, s, d), F32),
        scratch_shapes=[pltpu.VMEM((SUBLANES, d), F32), pltpu.VMEM((tm, d), BF16)],
        compiler_params=_DENSE_PARAMS,
        name="mixer",
    )(x, gain.reshape(1, d), w_in.astype(BF16), conv_w, w_out.astype(BF16))


def _ffn_kernel(*refs, has_attn, has_final):
    refs = list(refs)
    x_ref = refs.pop(0)
    att_ref, wo_ref = (refs.pop(0), refs.pop(0)) if has_attn else (None, None)
    g_ref, wup_ref, cw_ref, cb_ref, wdn_ref = (refs.pop(0) for _ in range(5))
    fg_ref = refs.pop(0) if has_final else None
    o_ref, carry_ref, act_ref = refs
    f = wdn_ref.shape[0]

    @pl.when(pl.program_id(1) == 0)
    def _():
        carry_ref[...] = jnp.zeros_like(carry_ref)

    x = x_ref[0]
    if has_attn:
        att = jnp.concatenate([att_ref[0, hp] for hp in range(att_ref.shape[1])], axis=1)
        x = x + _dot(att, wo_ref[...])
    xn = (_rms_scale(x) * g_ref[...]).astype(BF16)

    def conv_chunk(c0):
        cols = slice(c0, c0 + COL_CHUNK)
        u = _dot(xn, wup_ref[:, cols])
        return _causal_conv3(u, carry_ref, cw_ref, cols) + cb_ref[:, cols]

    for j in range(f // COL_CHUNK):
        gate = conv_chunk(j * COL_CHUNK)
        up = conv_chunk(f + j * COL_CHUNK)
        half = 0.5 * gate
        act = half * (1.0 + jnp.tanh(half)) * up
        act_ref[:, j * COL_CHUNK:(j + 1) * COL_CHUNK] = act.astype(BF16)
    y = x + _dot(act_ref[...], wdn_ref[...])
    if has_final:
        y = _rms_scale(y) * fg_ref[...]
    o_ref[0] = y


def _ffn(x, gain, w_up, conv_w, conv_b, w_down, att=None, w_o=None, final_gain=None):
    b, s, d = x.shape
    f = w_down.shape[0]
    tm = ROW_TILE
    has_attn, has_final = att is not None, final_gain is not None
    args, specs = [x], [_row_spec(tm, d)]
    if has_attn:
        args += [att, w_o.astype(BF16)]
        specs += [pl.BlockSpec((1, d // LANES, tm, LANES), lambda bb, i: (bb, 0, i, 0)), _resident((d, d))]
    args += [gain.reshape(1, d), w_up.astype(BF16), conv_w, conv_b.reshape(1, 2 * f), w_down.astype(BF16)]
    specs += [_resident((1, d)), _resident((d, 2 * f)), _resident((CONV_WIDTH, 2 * f)),
              _resident((1, 2 * f)), _resident((f, d))]
    if has_final:
        args.append(final_gain.reshape(1, d))
        specs.append(_resident((1, d)))
    return pl.pallas_call(
        functools.partial(_ffn_kernel, has_attn=has_attn, has_final=has_final),
        grid=(b, s // tm),
        in_specs=specs,
        out_specs=_row_spec(tm, d),
        out_shape=jax.ShapeDtypeStruct((b, s, d), F32),
        scratch_shapes=[pltpu.VMEM((SUBLANES, 2 * f), F32), pltpu.VMEM((tm, f), BF16)],
        compiler_params=_DENSE_PARAMS,
        name="ffn_attn" if has_attn else "ffn",
    )(*args)


DMAJ = 16


def _kvq_kernel(x_ref, gkv_ref, gq_ref, wkv_ref, wq_ref, *refs, scale):
    (q_nat, k_nat, v_nat), (q_maj, k_maj, v_maj) = refs[0:3], refs[3:6]
    tok_ref, mid_ref = refs[6:8]
    tm, d = x_ref.shape[1], x_ref.shape[2]
    step = 4
    y = _rms_scale(x_ref[0])
    xkv = (y * gkv_ref[...]).astype(BF16)
    xq = (y * gq_ref[...]).astype(BF16)
    head0 = lax.broadcasted_iota(jnp.int32, (1, LANES), 1) < LANES // 2
    slab = [0]

    def emit(val, pair, nat_ref, maj_ref):
        nat_ref[0, pair] = val.astype(BF16)
        i = slab[0] = (slab[0] + 1) % tok_ref.shape[0]
        tok_ref[i] = val
        for r in range(step):
            mid_ref[i, r * (tm // step):(r + 1) * (tm // step), :] = tok_ref[i, pl.ds(r, tm // step, stride=step), :]
        for r in range(DMAJ):
            rows = pl.ds((r % step) * (tm // step) + r // step, tm // DMAJ, stride=step)
            maj_ref[0, pair, r] = mid_ref[i, rows, :].astype(BF16)

    for j in range(d // COL_CHUNK):
        qc = _dot(xq, wq_ref[:, j * COL_CHUNK:(j + 1) * COL_CHUNK]) * scale
        kc = _dot(xkv, wkv_ref[:, j * COL_CHUNK:(j + 1) * COL_CHUNK])
        vc = _dot(xkv, wkv_ref[:, d + j * COL_CHUNK:d + (j + 1) * COL_CHUNK])
        for h in range(COL_CHUNK // LANES):
            pair = j * (COL_CHUNK // LANES) + h
            part = slice(h * LANES, (h + 1) * LANES)
            emit(qc[:, part], pair, q_nat, q_maj)
            emit(kc[:, part], pair, k_nat, k_maj)
            emit(vc[:, part], pair, v_nat, v_maj)


def _kvq(x, kv_gain, q_gain, w_kv, w_q, scale):
    b, s, d = x.shape
    tm = ROW_TILE
    pairs = d // LANES
    nat = jax.ShapeDtypeStruct((b, pairs, s, LANES), BF16)
    maj = jax.ShapeDtypeStruct((b, pairs, DMAJ, s // DMAJ, LANES), BF16)
    nat_spec = pl.BlockSpec((1, pairs, tm, LANES), lambda bb, i: (bb, 0, i, 0))
    maj_spec = pl.BlockSpec((1, pairs, DMAJ, tm // DMAJ, LANES), lambda bb, i: (bb, 0, 0, i, 0))
    return pl.pallas_call(
        functools.partial(_kvq_kernel, scale=scale),
        grid=(b, s // tm),
        in_specs=[_row_spec(tm, d), _resident((1, d)), _resident((1, d)),
                  _resident((d, 2 * d)), _resident((d, d))],
        out_specs=[nat_spec] * 3 + [maj_spec] * 3,
        out_shape=[nat] * 3 + [maj] * 3,
        scratch_shapes=[pltpu.VMEM((8, tm, LANES), F32)] * 2,
        compiler_params=_DENSE_PARAMS,
        name="kvq",
    )(x, kv_gain.reshape(1, d), q_gain.reshape(1, d), w_kv.astype(BF16), w_q.astype(BF16))


def _t5_bucket(dist):
    max_exact = REL_BUCKETS // 2
    n = jnp.maximum(dist, 0)
    nf = jnp.maximum(n, max_exact).astype(jnp.float32)
    large = max_exact + (jnp.log(nf / max_exact) / math.log(REL_MAX_DISTANCE / max_exact)
                         * (REL_BUCKETS - max_exact)).astype(jnp.int32)
    large = jnp.minimum(large, REL_BUCKETS - 1)
    return jnp.where(n < max_exact, n, large)


def _bias_table(rel_bias, qpos, kpos, dilation, window):
    delta = qpos[:, None] - kpos[None, :]
    band = (delta >= 0) & (delta <= window // dilation)
    onehot = jax.nn.one_hot(_t5_bucket(delta * dilation), REL_BUCKETS, dtype=F32)
    bias = jnp.einsum("qkn,nh->hqk", onehot, rel_bias.astype(F32), precision=lax.Precision.HIGHEST)
    return jnp.where(band[None], bias * LOG2E, MASKED)


def _bias_tables(rel_bias):
    p = BLOCK
    (w1, d1), (w4, d4), (w16, d16) = DILATED_BRANCHES
    runs = DMAJ // d4
    ar = lambda n: jnp.arange(n, dtype=jnp.int32)
    q_perm = (ar(p) % (p // runs)) * runs + ar(p) // (p // runs)
    k_perm_wide = (ar(2 * p) % (2 * p // runs)) * runs + ar(2 * p) // (2 * p // runs) - p
    wide = [
        _bias_table(rel_bias, ar(p), ar(2 * p) - p, d1, w1),
        _bias_table(rel_bias, ar(p), ar(2 * p), d1, w1),
        _bias_table(rel_bias, q_perm, k_perm_wide, d4, w4),
        _bias_table(rel_bias, ar(p), ar(2 * p) - p, d16, w16),
    ]
    narrow = [
        _bias_table(rel_bias, q_perm, q_perm, d4, w4),
        _bias_table(rel_bias, ar(p), ar(p), d16, w16),
    ]

    def pack(tabs):
        t = jnp.stack(tabs, axis=0)
        t = t.reshape(t.shape[0], N_HEADS // 2, 2, p, t.shape[-1])
        return t.transpose(1, 0, 2, 3, 4)

    return pack(wide), pack(narrow)


D1_INNER, D1_FIRST, D4_INNER, D16_INNER = range(4)
D4_FIRST, D16_FIRST = range(2)
ATTN_SLOTS = 2
ATTN_GROUP = 4


def _attn_kernel(zero_ref, qn_ref, kn_ref, vn_ref, qm_ref, km_ref, vm_ref, wide_ref, narrow_ref,
                 o_ref, acc_ref, den_ref, mx_ref, out_ref, s_ring, p_ring):
    s = qn_ref.shape[1]
    p = BLOCK
    (_, d1), (_, d4), (_, d16) = DILATED_BRANCHES
    runs = DMAJ // d4
    rl = p // runs
    quarter = s // d4
    head0 = lax.broadcasted_iota(jnp.int32, (1, LANES), 1) < LANES // 2
    one = jnp.ones((), BF16)

    zero = zero_ref[0]

    def ring(slot, u):
        return slot * 2 * ATTN_GROUP + u + zero

    def score_stage(blocks, slot):
        for i, (q, k, _, tab_ref, kind, _, _) in enumerate(blocks):
            qb, kw = q(), k()
            for hh in range(2):
                qh = jnp.where(head0 if hh == 0 else jnp.logical_not(head0), qb, jnp.zeros_like(qb))
                sc = lax.dot_general(qh, kw, (((1,), (1,)), ((), ())), preferred_element_type=F32)
                s_ring[ring(slot, 2 * i + hh), :, 0:kw.shape[0]] = sc + tab_ref[0, kind, hh]

    def softmax_stage(blocks, slot):
        for i, (_, k, _, _, _, put, _) in enumerate(blocks):
            maxes = []
            for u in (2 * i, 2 * i + 1):
                sc = s_ring[ring(slot, u), :, 0:k.width]
                m = jnp.max(sc, axis=-1, keepdims=True)
                p_ring[ring(slot, u), :, 0:k.width] = jnp.exp2(sc - m).astype(BF16)
                maxes.append(jnp.broadcast_to(m, (p, LANES)))
            put(mx_ref, jnp.where(head0, maxes[0], maxes[1]))

    def value_stage(blocks, slot):
        for i, (_, k, v, _, _, put, after) in enumerate(blocks):
            vw = v()
            o0 = _dot(p_ring[ring(slot, 2 * i), :, 0:k.width], jnp.where(head0, vw, one))
            o1 = _dot(p_ring[ring(slot, 2 * i + 1), :, 0:k.width], jnp.where(head0, one, vw))
            put(acc_ref, jnp.where(head0, o0, o1))
            put(den_ref, pltpu.roll(jnp.where(head0, o1, o0), LANES // 2, 1))
            if after is not None:
                after()

    def window(fn, width):
        fn.width = width
        return fn

    def put_rows(bi, rows):
        def put(ref, val):
            ref[bi, rows, :] = val
        return put

    blocks1, blocks16, blocks4 = [], [], {}
    for g in range(s // p):
        row0, win = g * p, slice(max(g - 1, 0) * p, max(g - 1, 0) * p + 2 * p)
        blocks1.append((
            lambda row0=row0: qn_ref[0, row0:row0 + p, :],
            window(lambda win=win: kn_ref[0, win, :], 2 * p),
            lambda win=win: vn_ref[0, win, :],
            wide_ref, D1_FIRST if g == 0 else D1_INNER, put_rows(0, pl.ds(row0, p)), None))
    for r in range(d16):
        base = (r % d4) * quarter + r // d4
        for j in range(s // d16 // p):
            win = slice(max(j - 1, 0) * p, (j + 1) * p)
            blocks16.append((
                lambda r=r, j=j: qm_ref[0, r, j * p:(j + 1) * p, :],
                window(lambda r=r, win=win: km_ref[0, r, win, :], win.stop - win.start),
                lambda r=r, win=win: vm_ref[0, r, win, :],
                narrow_ref if j == 0 else wide_ref, D16_FIRST if j == 0 else D16_INNER,
                put_rows(2, pl.ds(base + j * p * (d16 // d4), p, stride=d16 // d4)), None))
    for r4 in range(d4):
        def gather(ref, lo, n, r4=r4):
            return jnp.concatenate([ref[0, c * d4 + r4, lo:lo + n, :] for c in range(runs)], axis=0)

        def put_runs(j, r4=r4):
            def put(ref, val):
                for c in range(runs):
                    ref[1, pl.ds(r4 * quarter + j * p + c, rl, stride=runs), :] = val[c * rl:(c + 1) * rl]
            return put

        for j in range(s // d4 // p):
            lo = max(j - 1, 0) * rl
            width = rl if j == 0 else 2 * rl
            blocks4[r4, j] = (
                lambda j=j, gather=gather: gather(qm_ref, j * rl, rl),
                window(lambda lo=lo, width=width, gather=gather: gather(km_ref, lo, width), runs * width),
                lambda lo=lo, width=width, gather=gather: gather(vm_ref, lo, width),
                narrow_ref if j == 0 else wide_ref, D4_FIRST if j == 0 else D4_INNER, put_runs(j), None)

    blocks = blocks1 + blocks16 + [blocks4[r4, j] for r4 in range(d4) for j in range(s // d4 // p)]

    pending = []
    for n, g0 in enumerate(range(0, len(blocks), ATTN_GROUP)):
        group = blocks[g0:g0 + ATTN_GROUP]
        score_stage(group, n % ATTN_SLOTS)
        if len(pending) == ATTN_SLOTS - 1:
            value_stage(*pending.pop(0))
        softmax_stage(group, n % ATTN_SLOTS)
        pending.append((group, n % ATTN_SLOTS))
    for item in pending:
        value_stage(*item)

    mt = 4 * p
    per_quarter = quarter // mt

    def merge_body(i, carry):
        r4 = i // per_quarter
        a = (i % per_quarter) * mt
        rows_maj = pl.ds(pl.multiple_of(r4 * quarter + a, mt), mt)
        rows_tok = pl.ds(a * d4 + r4, mt, stride=d4)
        rows = (rows_tok, rows_maj, rows_maj)
        ms = [mx_ref[bi, rows[bi], :] for bi in range(3)]
        m_all = functools.reduce(jnp.maximum, ms)
        num = jnp.zeros((mt, LANES), F32)
        den = jnp.zeros((mt, LANES), F32)
        for bi in range(3):
            w = jnp.exp2(ms[bi] - m_all)
            num = num + w * acc_ref[bi, rows[bi], :]
            den = den + w * den_ref[bi, rows[bi], :]
        out_ref[rows_tok, :] = num / den
        return carry

    lax.fori_loop(0, s // mt, merge_body, 0)

    def cast_body(i, carry):
        rows = pl.ds(pl.multiple_of(i * mt, mt), mt)
        o_ref[0, rows, :] = out_ref[rows, :].astype(o_ref.dtype)
        return carry

    lax.fori_loop(0, s // mt, cast_body, 0)


def _attention(nat, maj, wide, narrow):
    b, pairs, s, _ = nat[0].shape
    d = pairs * LANES
    nat_spec = pl.BlockSpec((1, None, s, LANES), lambda hp, bb: (bb, hp, 0, 0))
    maj_spec = pl.BlockSpec((1, None, DMAJ, s // DMAJ, LANES), lambda hp, bb: (bb, hp, 0, 0, 0))
    tab_spec = lambda t: pl.BlockSpec((1,) + t.shape[1:], lambda hp, bb: (hp, 0, 0, 0, 0))
    per_branch = pltpu.VMEM((len(DILATED_BRANCHES), s, LANES), F32)
    return pl.pallas_call(
        _attn_kernel,
        grid=(d // LANES, b),
        in_specs=[pl.BlockSpec(memory_space=pltpu.SMEM)] + [nat_spec] * 3 + [maj_spec] * 3
        + [tab_spec(wide), tab_spec(narrow)],
        out_specs=nat_spec,
        out_shape=jax.ShapeDtypeStruct((b, pairs, s, LANES), BF16),
        scratch_shapes=[per_branch] * 3 + [
            pltpu.VMEM((s, LANES), F32),
            pltpu.VMEM((ATTN_SLOTS * 2 * ATTN_GROUP, BLOCK, 2 * BLOCK), F32),
            pltpu.VMEM((ATTN_SLOTS * 2 * ATTN_GROUP, BLOCK, 2 * BLOCK), BF16),
        ],
        compiler_params=pltpu.CompilerParams(
            dimension_semantics=("arbitrary", "arbitrary"), vmem_limit_bytes=VMEM_LIMIT),
        name="dilated_attn",
    )(jnp.zeros((1,), jnp.int32), *nat, *maj, wide, narrow)


def kernel(x, a_norm, a_w_in, a_conv, a_w_out, kv_norm, w_kv, b_norm, b_w_q, b_w_o, rel_bias,
           ffn_norm, ffn_w_up, ffn_conv, ffn_conv_b, ffn_w_down, final_norm):
    d = x.shape[-1]
    n_a, n_b = a_norm.shape[0], b_norm.shape[0]
    depth = n_a + n_b
    assert n_b == 1, "the shared K/V and the query projection are fused for a single attention layer"
    scale = (d // N_HEADS) ** -0.5 * LOG2E
    wide, narrow = _bias_tables(rel_bias)

    h = x
    for l in range(n_a):
        h = _mixer(h, a_norm[l], a_w_in[l], a_conv[l], a_w_out[l])
        h = _ffn(h, ffn_norm[l], ffn_w_up[l], ffn_conv[l], ffn_conv_b[l], ffn_w_down[l])
    outs = _kvq(h, kv_norm, b_norm[0], w_kv, b_w_q[0], scale)
    att = _attention(outs[:3], outs[3:], wide, narrow)
    l = depth - 1
    return _ffn(h, ffn_norm[l], ffn_w_up[l], ffn_conv[l], ffn_conv_b[l], ffn_w_down[l],
                att=att, w_o=b_w_o[0], final_gain=final_norm)
```

```python
import functools
import math

import jax
import jax.numpy as jnp
from jax import lax
from jax.experimental import pallas as pl
from jax.experimental.pallas import tpu as pltpu

N_HEADS = 16
CONV_WIDTH = 3
DILATED_BRANCHES = ((128, 1), (512, 4), (2048, 16))
BLOCK = 128
REL_BUCKETS = 32
REL_MAX_DISTANCE = 2048
RMS_EPS = 1e-6

LANES = 128
SUBLANES = 8
ROW_TILE = 1024
COL_CHUNK = 256
VMEM_LIMIT = 56 * 1024 * 1024
MASKED = -1e30
LOG2E = math.log2(math.e)

BF16 = jnp.bfloat16
F32 = jnp.float32


def _dot(a, b):
    return jnp.dot(a, b, preferred_element_type=F32)


def _rms_scale(x):
    return x * lax.rsqrt(jnp.mean(x * x, axis=-1, keepdims=True) + RMS_EPS)


def _shift_rows(cur, prev, k):
    rolled = pltpu.roll(cur, k, 0)
    top = pltpu.roll(prev, k, 0)
    row = lax.broadcasted_iota(jnp.int32, prev.shape, 0)
    first = jnp.where(row < k, top, rolled[:SUBLANES])
    return jnp.concatenate([first, rolled[SUBLANES:]], axis=0)


def _causal_conv3(u, carry_ref, w_ref, cols):
    prev = carry_ref[:, cols]
    carry_ref[:, cols] = u[u.shape[0] - SUBLANES:]
    u1 = _shift_rows(u, prev, 1)
    u2 = _shift_rows(u, prev, 2)
    return w_ref[0:1, cols] * u2 + w_ref[1:2, cols] * u1 + w_ref[2:3, cols] * u


def _resident(shape):
    return pl.BlockSpec(shape, lambda b, i: (0,) * len(shape), pipeline_mode=pl.Buffered(1))


def _row_spec(tm, width):
    return pl.BlockSpec((1, tm, width), lambda b, i: (b, i, 0))


_DENSE_PARAMS = pltpu.CompilerParams(
    dimension_semantics=("arbitrary", "arbitrary"), vmem_limit_bytes=VMEM_LIMIT)


def _mixer_kernel(x_ref, g_ref, win_ref, cw_ref, wout_ref, o_ref, carry_ref, y_ref):
    d = x_ref.shape[2]

    @pl.when(pl.program_id(1) == 0)
    def _():
        carry_ref[...] = jnp.zeros_like(carry_ref)

    x = x_ref[0]
    xn = (_rms_scale(x) * g_ref[...]).astype(BF16)

    for j in range(d // COL_CHUNK):
        cols = slice(j * COL_CHUNK, (j + 1) * COL_CHUNK)
        gate_b = _dot(xn, win_ref[:, cols])
        gate_c = _dot(xn, win_ref[:, d + j * COL_CHUNK:d + (j + 1) * COL_CHUNK])
        hh = _dot(xn, win_ref[:, 2 * d + j * COL_CHUNK:2 * d + (j + 1) * COL_CHUNK])
        ch = gate_c * hh
        conv = _causal_conv3(ch, carry_ref, cw_ref, cols)
        y_ref[:, cols] = (gate_b * conv).astype(BF16)
    o_ref[0] = x + _dot(y_ref[...], wout_ref[...])


def _mixer(x, gain, w_in, conv_w, w_out):
    b, s, d = x.shape
    tm = ROW_TILE
    return pl.pallas_call(
        _mixer_kernel,
        grid=(b, s // tm),
        in_specs=[_row_spec(tm, d), _resident((1, d)), _resident((d, 3 * d)),
                  _resident((CONV_WIDTH, d)), _resident((d, d))],
        out_specs=_row_spec(tm, d),
        out_shape=jax.ShapeDtypeStruct((b, s, d), F32),
        scratch_shapes=[pltpu.VMEM((SUBLANES, d), F32), pltpu.VMEM((tm, d), BF16)],
        compiler_params=_DENSE_PARAMS,
        name="mixer",
    )(x, gain.reshape(1, d), w_in.astype(BF16), conv_w, w_out.astype(BF16))


def _ffn_kernel(*refs, has_attn, has_final):
    refs = list(refs)
    x_ref = refs.pop(0)
    att_ref, wo_ref = (refs.pop(0), refs.pop(0)) if has_attn else (None, None)
    g_ref, wup_ref, cw_ref, cb_ref, wdn_ref = (refs.pop(0) for _ in range(5))
    fg_ref = refs.pop(0) if has_final else None
    o_ref, carry_ref, act_ref = refs
    f = wdn_ref.shape[0]

    @pl.when(pl.program_id(1) == 0)
    def _():
        carry_ref[...] = jnp.zeros_like(carry_ref)

    x = x_ref[0]
    if has_attn:
        att = jnp.concatenate([att_ref[0, hp] for hp in range(att_ref.shape[1])], axis=1)
        x = x + _dot(att, wo_ref[...])
    xn = (_rms_scale(x) * g_ref[...]).astype(BF16)

    def conv_chunk(c0):
        cols = slice(c0, c0 + COL_CHUNK)
        u = _dot(xn, wup_ref[:, cols])
        return _causal_conv3(u, carry_ref, cw_ref, cols) + cb_ref[:, cols]

    for j in range(f // COL_CHUNK):
        gate = conv_chunk(j * COL_CHUNK)
        up = conv_chunk(f + j * COL_CHUNK)
        half = 0.5 * gate
        act = half * (1.0 + jnp.tanh(half)) * up
        act_ref[:, j * COL_CHUNK:(j + 1) * COL_CHUNK] = act.astype(BF16)
    y = x + _dot(act_ref[...], wdn_ref[...])
    if has_final:
        y = _rms_scale(y) * fg_ref[...]
    o_ref[0] = y


def _ffn(x, gain, w_up, conv_w, conv_b, w_down, att=None, w_o=None, final_gain=None):
    b, s, d = x.shape
    f = w_down.shape[0]
    tm = ROW_TILE
    has_attn, has_final = att is not None, final_gain is not None
    args, specs = [x], [_row_spec(tm, d)]
    if has_attn:
        args += [att, w_o.astype(BF16)]
        specs += [pl.BlockSpec((1, d // LANES, tm, LANES), lambda bb, i: (bb, 0, i, 0)), _resident((d, d))]
    args += [gain.reshape(1, d), w_up.astype(BF16), conv_w, conv_b.reshape(1, 2 * f), w_down.astype(BF16)]
    specs += [_resident((1, d)), _resident((d, 2 * f)), _resident((CONV_WIDTH, 2 * f)),
              _resident((1, 2 * f)), _resident((f, d))]
    if has_final:
        args.append(final_gain.reshape(1, d))
        specs.append(_resident((1, d)))
    return pl.pallas_call(
        functools.partial(_ffn_kernel, has_attn=has_attn, has_final=has_final),
        grid=(b, s // tm),
        in_specs=specs,
        out_specs=_row_spec(tm, d),
        out_shape=jax.ShapeDtypeStruct((b, s, d), F32),
        scratch_shapes=[pltpu.VMEM((SUBLANES, 2 * f), F32), pltpu.VMEM((tm, f), BF16)],
        compiler_params=_DENSE_PARAMS,
        name="ffn_attn" if has_attn else "ffn",
    )(*args)


DMAJ = 16


def _kvq_kernel(x_ref, gkv_ref, gq_ref, wkv_ref, wq_ref, *refs, scale):
    (q_nat, k_nat, v_nat), (q_maj, k_maj, v_maj) = refs[0:3], refs[3:6]
    tok_ref, mid_ref = refs[6:8]
    tm, d = x_ref.shape[1], x_ref.shape[2]
    step = 4
    y = _rms_scale(x_ref[0])
    xkv = (y * gkv_ref[...]).astype(BF16)
    xq = (y * gq_ref[...]).astype(BF16)
    head0 = lax.broadcasted_iota(jnp.int32, (1, LANES), 1) < LANES // 2
    slab = [0]

    def emit(val, pair, nat_ref, maj_ref):
        nat_ref[0, pair] = val.astype(BF16)
        i = slab[0] = (slab[0] + 1) % tok_ref.shape[0]
        tok_ref[i] = val
        for r in range(step):
            mid_ref[i, r * (tm // step):(r + 1) * (tm // step), :] = tok_ref[i, pl.ds(r, tm // step, stride=step), :]
        for r in range(DMAJ):
            rows = pl.ds((r % step) * (tm // step) + r // step, tm // DMAJ, stride=step)
            maj_ref[0, pair, 0, r] = mid_ref[i, rows, :].astype(BF16)

    for j in range(d // COL_CHUNK):
        qc = _dot(xq, wq_ref[:, j * COL_CHUNK:(j + 1) * COL_CHUNK]) * scale
        kc = _dot(xkv, wkv_ref[:, j * COL_CHUNK:(j + 1) * COL_CHUNK])
        vc = _dot(xkv, wkv_ref[:, d + j * COL_CHUNK:d + (j + 1) * COL_CHUNK])
        for h in range(COL_CHUNK // LANES):
            pair = j * (COL_CHUNK // LANES) + h
            part = slice(h * LANES, (h + 1) * LANES)
            emit(qc[:, part], pair, q_nat, q_maj)
            emit(kc[:, part], pair, k_nat, k_maj)
            emit(vc[:, part], pair, v_nat, v_maj)


def _kvq(x, kv_gain, q_gain, w_kv, w_q, scale):
    b, s, d = x.shape
    tm = ROW_TILE
    pairs = d // LANES
    nat = jax.ShapeDtypeStruct((b, pairs, s, LANES), BF16)
    maj = jax.ShapeDtypeStruct((b, pairs, s // tm, DMAJ, tm // DMAJ, LANES), BF16)
    nat_spec = pl.BlockSpec((1, pairs, tm, LANES), lambda bb, i: (bb, 0, i, 0))
    maj_spec = pl.BlockSpec((1, pairs, 1, DMAJ, tm // DMAJ, LANES), lambda bb, i: (bb, 0, i, 0, 0, 0))
    return pl.pallas_call(
        functools.partial(_kvq_kernel, scale=scale),
        grid=(b, s // tm),
        in_specs=[_row_spec(tm, d), _resident((1, d)), _resident((1, d)),
                  _resident((d, 2 * d)), _resident((d, d))],
        out_specs=[nat_spec] * 3 + [maj_spec] * 3,
        out_shape=[nat] * 3 + [maj] * 3,
        scratch_shapes=[pltpu.VMEM((8, tm, LANES), F32)] * 2,
        compiler_params=_DENSE_PARAMS,
        name="kvq",
    )(x, kv_gain.reshape(1, d), q_gain.reshape(1, d), w_kv.astype(BF16), w_q.astype(BF16))


def _t5_bucket(dist):
    max_exact = REL_BUCKETS // 2
    n = jnp.maximum(dist, 0)
    nf = jnp.maximum(n, max_exact).astype(jnp.float32)
    large = max_exact + (jnp.log(nf / max_exact) / math.log(REL_MAX_DISTANCE / max_exact)
                         * (REL_BUCKETS - max_exact)).astype(jnp.int32)
    large = jnp.minimum(large, REL_BUCKETS - 1)
    return jnp.where(n < max_exact, n, large)


def _bias_table(rel_bias, qpos, kpos, dilation, window):
    delta = qpos[:, None] - kpos[None, :]
    band = (delta >= 0) & (delta <= window // dilation)
    onehot = jax.nn.one_hot(_t5_bucket(delta * dilation), REL_BUCKETS, dtype=F32)
    bias = jnp.einsum("qkn,nh->hqk", onehot, rel_bias.astype(F32), precision=lax.Precision.HIGHEST)
    return jnp.where(band[None], bias * LOG2E, MASKED)


def _bias_tables(rel_bias):
    p = BLOCK
    (w1, d1), (w4, d4), (w16, d16) = DILATED_BRANCHES
    runs = DMAJ // d4
    ar = lambda n: jnp.arange(n, dtype=jnp.int32)
    q_perm = (ar(p) % (p // runs)) * runs + ar(p) // (p // runs)
    k_perm_wide = (ar(2 * p) % (2 * p // runs)) * runs + ar(2 * p) // (2 * p // runs) - p
    wide = [
        _bias_table(rel_bias, ar(p), ar(2 * p) - p, d1, w1),
        _bias_table(rel_bias, ar(p), ar(2 * p), d1, w1),
        _bias_table(rel_bias, q_perm, k_perm_wide, d4, w4),
        _bias_table(rel_bias, ar(p), ar(2 * p) - p, d16, w16),
    ]
    narrow = [
        _bias_table(rel_bias, q_perm, q_perm, d4, w4),
        _bias_table(rel_bias, ar(p), ar(p), d16, w16),
    ]

    def pack(tabs):
        t = jnp.stack(tabs, axis=0)
        t = t.reshape(t.shape[0], N_HEADS // 2, 2, p, t.shape[-1])
        return t.transpose(1, 0, 2, 3, 4)

    return pack(wide), pack(narrow)


D1_INNER, D1_FIRST, D4_INNER, D16_INNER = range(4)
D4_FIRST, D16_FIRST = range(2)
ATTN_SLOTS = 2
ATTN_GROUP = 4


def _attn_kernel(zero_ref, qn_ref, kn_ref, vn_ref, qm_ref, km_ref, vm_ref, wide_ref, narrow_ref,
                 o_ref, acc_ref, den_ref, mx_ref, out_ref, s_ring, p_ring):
    s = qn_ref.shape[1]
    p = BLOCK
    (_, d1), (_, d4), (_, d16) = DILATED_BRANCHES
    runs = DMAJ // d4
    rl = p // runs
    quarter = s // d4
    head0 = lax.broadcasted_iota(jnp.int32, (1, LANES), 1) < LANES // 2
    one = jnp.ones((), BF16)

    zero = zero_ref[0]

    def ring(slot, u):
        return slot * 2 * ATTN_GROUP + u + zero

    def score_stage(blocks, slot):
        for i, (q, k, _, tab_ref, kind, _, _) in enumerate(blocks):
            qb, kw = q(), k()
            for hh in range(2):
                qh = jnp.where(head0 if hh == 0 else jnp.logical_not(head0), qb, jnp.zeros_like(qb))
                sc = lax.dot_general(qh, kw, (((1,), (1,)), ((), ())), preferred_element_type=F32)
                s_ring[ring(slot, 2 * i + hh), :, 0:kw.shape[0]] = sc + tab_ref[0, kind, hh]

    def softmax_stage(blocks, slot):
        for i, (_, k, _, _, _, put, _) in enumerate(blocks):
            maxes = []
            for u in (2 * i, 2 * i + 1):
                sc = s_ring[ring(slot, u), :, 0:k.width]
                m = jnp.max(sc, axis=-1, keepdims=True)
                p_ring[ring(slot, u), :, 0:k.width] = jnp.exp2(sc - m).astype(BF16)
                maxes.append(jnp.broadcast_to(m, (p, LANES)))
            put(mx_ref, jnp.where(head0, maxes[0], maxes[1]))

    def value_stage(blocks, slot):
        for i, (_, k, v, _, _, put, after) in enumerate(blocks):
            vw = v()
            o0 = _dot(p_ring[ring(slot, 2 * i), :, 0:k.width], jnp.where(head0, vw, one))
            o1 = _dot(p_ring[ring(slot, 2 * i + 1), :, 0:k.width], jnp.where(head0, one, vw))
            put(acc_ref, jnp.where(head0, o0, o1))
            put(den_ref, pltpu.roll(jnp.where(head0, o1, o0), LANES // 2, 1))
            if after is not None:
                after()

    def window(fn, width):
        fn.width = width
        return fn

    def maj_rows(ref, r, lo, n):
        tile_len = ref.shape[3]
        pieces = []
        while n > 0:
            t, off = divmod(lo, tile_len)
            m = min(n, tile_len - off)
            pieces.append(ref[0, t, r, off:off + m, :])
            lo, n = lo + m, n - m
        return pieces[0] if len(pieces) == 1 else jnp.concatenate(pieces, axis=0)

    def put_rows(bi, rows):
        def put(ref, val):
            ref[bi, rows, :] = val
        return put

    blocks1, blocks16, blocks4 = [], [], {}
    for g in range(s // p):
        row0, win = g * p, slice(max(g - 1, 0) * p, max(g - 1, 0) * p + 2 * p)
        blocks1.append((
            lambda row0=row0: qn_ref[0, row0:row0 + p, :],
            window(lambda win=win: kn_ref[0, win, :], 2 * p),
            lambda win=win: vn_ref[0, win, :],
            wide_ref, D1_FIRST if g == 0 else D1_INNER, put_rows(0, pl.ds(row0, p)), None))
    for r in range(d16):
        base = (r % d4) * quarter + r // d4
        for j in range(s // d16 // p):
            win = slice(max(j - 1, 0) * p, (j + 1) * p)
            blocks16.append((
                lambda r=r, j=j: maj_rows(qm_ref, r, j * p, p),
                window(lambda r=r, win=win: maj_rows(km_ref, r, win.start, win.stop - win.start),
                       win.stop - win.start),
                lambda r=r, win=win: maj_rows(vm_ref, r, win.start, win.stop - win.start),
                narrow_ref if j == 0 else wide_ref, D16_FIRST if j == 0 else D16_INNER,
                put_rows(2, pl.ds(base + j * p * (d16 // d4), p, stride=d16 // d4)), None))
    for r4 in range(d4):
        def gather(ref, lo, n, r4=r4):
            return jnp.concatenate([maj_rows(ref, c * d4 + r4, lo, n) for c in range(runs)], axis=0)

        def put_runs(j, r4=r4):
            def put(ref, val):
                for c in range(runs):
                    ref[1, pl.ds(r4 * quarter + j * p + c, rl, stride=runs), :] = val[c * rl:(c + 1) * rl]
            return put

        for j in range(s // d4 // p):
            lo = max(j - 1, 0) * rl
            width = rl if j == 0 else 2 * rl
            blocks4[r4, j] = (
                lambda j=j, gather=gather: gather(qm_ref, j * rl, rl),
                window(lambda lo=lo, width=width, gather=gather: gather(km_ref, lo, width), runs * width),
                lambda lo=lo, width=width, gather=gather: gather(vm_ref, lo, width),
                narrow_ref if j == 0 else wide_ref, D4_FIRST if j == 0 else D4_INNER, put_runs(j), None)

    blocks = blocks1 + blocks16 + [blocks4[r4, j] for r4 in range(d4) for j in range(s // d4 // p)]

    pending = []
    for n, g0 in enumerate(range(0, len(blocks), ATTN_GROUP)):
        group = blocks[g0:g0 + ATTN_GROUP]
        score_stage(group, n % ATTN_SLOTS)
        if len(pending) == ATTN_SLOTS - 1:
            value_stage(*pending.pop(0))
        softmax_stage(group, n % ATTN_SLOTS)
        pending.append((group, n % ATTN_SLOTS))
    for item in pending:
        value_stage(*item)

    mt = 4 * p
    per_quarter = quarter // mt

    def merge_body(i, carry):
        r4 = i // per_quarter
        a = (i % per_quarter) * mt
        rows_maj = pl.ds(pl.multiple_of(r4 * quarter + a, mt), mt)
        rows_tok = pl.ds(a * d4 + r4, mt, stride=d4)
        rows = (rows_tok, rows_maj, rows_maj)
        ms = [mx_ref[bi, rows[bi], :] for bi in range(3)]
        m_all = functools.reduce(jnp.maximum, ms)
        ws = [jnp.exp2(m - m_all) for m in ms]
        num = functools.reduce(jnp.add, [w * acc_ref[bi, rows[bi], :] for bi, w in enumerate(ws)])
        den = functools.reduce(jnp.add, [w * den_ref[bi, rows[bi], :] for bi, w in enumerate(ws)])
        out_ref[rows_tok, :] = num / den
        return carry

    lax.fori_loop(0, s // mt, merge_body, 0)

    def cast_body(i, carry):
        rows = pl.ds(pl.multiple_of(i * mt, mt), mt)
        o_ref[0, rows, :] = out_ref[rows, :].astype(o_ref.dtype)
        return carry

    lax.fori_loop(0, s // mt, cast_body, 0)


def _attention(nat, maj, wide, narrow):
    b, pairs, s, _ = nat[0].shape
    d = pairs * LANES
    nat_spec = pl.BlockSpec((1, None, s, LANES), lambda hp, bb: (bb, hp, 0, 0))
    maj_spec = pl.BlockSpec((1, None) + maj[0].shape[2:], lambda hp, bb: (bb, hp, 0, 0, 0, 0))
    tab_spec = lambda t: pl.BlockSpec((1,) + t.shape[1:], lambda hp, bb: (hp, 0, 0, 0, 0))
    per_branch = pltpu.VMEM((len(DILATED_BRANCHES), s, LANES), F32)
    return pl.pallas_call(
        _attn_kernel,
        grid=(d // LANES, b),
        in_specs=[pl.BlockSpec(memory_space=pltpu.SMEM)] + [nat_spec] * 3 + [maj_spec] * 3
        + [tab_spec(wide), tab_spec(narrow)],
        out_specs=nat_spec,
        out_shape=jax.ShapeDtypeStruct((b, pairs, s, LANES), BF16),
        scratch_shapes=[per_branch] * 3 + [
            pltpu.VMEM((s, LANES), F32),
            pltpu.VMEM((ATTN_SLOTS * 2 * ATTN_GROUP, BLOCK, 2 * BLOCK), F32),
            pltpu.VMEM((ATTN_SLOTS * 2 * ATTN_GROUP, BLOCK, 2 * BLOCK), BF16),
        ],
        compiler_params=pltpu.CompilerParams(
            dimension_semantics=("arbitrary", "arbitrary"), vmem_limit_bytes=VMEM_LIMIT),
        name="dilated_attn",
    )(jnp.zeros((1,), jnp.int32), *nat, *maj, wide, narrow)


def kernel(x, a_norm, a_w_in, a_conv, a_w_out, kv_norm, w_kv, b_norm, b_w_q, b_w_o, rel_bias,
           ffn_norm, ffn_w_up, ffn_conv, ffn_conv_b, ffn_w_down, final_norm):
    d = x.shape[-1]
    n_a, n_b = a_norm.shape[0], b_norm.shape[0]
    depth = n_a + n_b
    assert n_b == 1, "the shared K/V and the query projection are fused for a single attention layer"
    scale = (d // N_HEADS) ** -0.5 * LOG2E
    wide, narrow = _bias_tables(rel_bias)

    h = x
    for l in range(n_a):
        h = _mixer(h, a_norm[l], a_w_in[l], a_conv[l], a_w_out[l])
        h = _ffn(h, ffn_norm[l], ffn_w_up[l], ffn_conv[l], ffn_conv_b[l], ffn_w_down[l])
    outs = _kvq(h, kv_norm, b_norm[0], w_kv, b_w_q[0], scale)
    att = _attention(outs[:3], outs[3:], wide, narrow)
    l = depth - 1
    return _ffn(h, ffn_norm[l], ffn_w_up[l], ffn_conv[l], ffn_conv_b[l], ffn_w_down[l],
                att=att, w_o=b_w_o[0], final_gain=final_norm)
```

```python
import functools
import math

import jax
import jax.numpy as jnp
from jax import lax
from jax.experimental import pallas as pl
from jax.experimental.pallas import tpu as pltpu

N_HEADS = 16
CONV_WIDTH = 3
DILATED_BRANCHES = ((128, 1), (512, 4), (2048, 16))
BLOCK = 128
REL_BUCKETS = 32
REL_MAX_DISTANCE = 2048
RMS_EPS = 1e-6

LANES = 128
SUBLANES = 8
ROW_TILE = 1024
COL_CHUNK = 256
VMEM_LIMIT = 56 * 1024 * 1024
MASKED = -1e30
LOG2E = math.log2(math.e)

BF16 = jnp.bfloat16
F32 = jnp.float32


def _dot(a, b):
    return jnp.dot(a, b, preferred_element_type=F32)


def _rms_scale(x):
    return x * lax.rsqrt(jnp.mean(x * x, axis=-1, keepdims=True) + RMS_EPS)


def _shift_rows(cur, prev, k):
    rolled = pltpu.roll(cur, k, 0)
    top = pltpu.roll(prev, k, 0)
    row = lax.broadcasted_iota(jnp.int32, prev.shape, 0)
    first = jnp.where(row < k, top, rolled[:SUBLANES])
    return jnp.concatenate([first, rolled[SUBLANES:]], axis=0)


def _causal_conv3(u, carry_ref, w_ref, cols):
    prev = carry_ref[:, cols]
    carry_ref[:, cols] = u[u.shape[0] - SUBLANES:]
    u1 = _shift_rows(u, prev, 1)
    u2 = _shift_rows(u, prev, 2)
    return w_ref[0:1, cols] * u2 + w_ref[1:2, cols] * u1 + w_ref[2:3, cols] * u


def _resident(shape):
    return pl.BlockSpec(shape, lambda b, i: (0,) * len(shape), pipeline_mode=pl.Buffered(1))


def _row_spec(tm, width):
    return pl.BlockSpec((1, tm, width), lambda b, i: (b, i, 0))


_DENSE_PARAMS = pltpu.CompilerParams(
    dimension_semantics=("arbitrary", "arbitrary"), vmem_limit_bytes=VMEM_LIMIT)


def _mixer_kernel(x_ref, g_ref, win_ref, cw_ref, wout_ref, o_ref, carry_ref, y_ref):
    d = x_ref.shape[2]

    @pl.when(pl.program_id(1) == 0)
    def _():
        carry_ref[...] = jnp.zeros_like(carry_ref)

    x = x_ref[0]
    xn = (_rms_scale(x) * g_ref[...]).astype(BF16)

    for j in range(d // COL_CHUNK):
        cols = slice(j * COL_CHUNK, (j + 1) * COL_CHUNK)
        gate_b = _dot(xn, win_ref[:, cols])
        gate_c = _dot(xn, win_ref[:, d + j * COL_CHUNK:d + (j + 1) * COL_CHUNK])
        hh = _dot(xn, win_ref[:, 2 * d + j * COL_CHUNK:2 * d + (j + 1) * COL_CHUNK])
        ch = gate_c * hh
        conv = _causal_conv3(ch, carry_ref, cw_ref, cols)
        y_ref[:, cols] = (gate_b * conv).astype(BF16)
    o_ref[0] = x + _dot(y_ref[...], wout_ref[...])


def _mixer(x, gain, w_in, conv_w, w_out):
    b, s, d = x.shape
    tm = ROW_TILE
    return pl.pallas_call(
        _mixer_kernel,
        grid=(b, s // tm),
        in_specs=[_row_spec(tm, d), _resident((1, d)), _resident((d, 3 * d)),
                  _resident((CONV_WIDTH, d)), _resident((d, d))],
        out_specs=_row_spec(tm, d),
        out_shape=jax.ShapeDtypeStruct((b, s, d), F32),
        scratch_shapes=[pltpu.VMEM((SUBLANES, d), F32), pltpu.VMEM((tm, d), BF16)],
        compiler_params=_DENSE_PARAMS,
        name="mixer",
    )(x, gain.reshape(1, d), w_in.astype(BF16), conv_w, w_out.astype(BF16))


def _ffn_kernel(*refs, has_attn, has_final):
    refs = list(refs)
    x_ref = refs.pop(0)
    att_ref, wo_ref = (refs.pop(0), refs.pop(0)) if has_attn else (None, None)
    g_ref, wup_ref, cw_ref, cb_ref, wdn_ref = (refs.pop(0) for _ in range(5))
    fg_ref = refs.pop(0) if has_final else None
    o_ref, carry_ref, act_ref = refs
    f = wdn_ref.shape[0]

    @pl.when(pl.program_id(1) == 0)
    def _():
        carry_ref[...] = jnp.zeros_like(carry_ref)

    x = x_ref[0]
    if has_attn:
        att = jnp.concatenate([att_ref[0, hp] for hp in range(att_ref.shape[1])], axis=1)
        x = x + _dot(att, wo_ref[...])
    xn = (_rms_scale(x) * g_ref[...]).astype(BF16)

    def conv_chunk(c0):
        cols = slice(c0, c0 + COL_CHUNK)
        u = _dot(xn, wup_ref[:, cols])
        return _causal_conv3(u, carry_ref, cw_ref, cols) + cb_ref[:, cols]

    for j in range(f // COL_CHUNK):
        gate = conv_chunk(j * COL_CHUNK)
        up = conv_chunk(f + j * COL_CHUNK)
        half = 0.5 * gate
        act = half * (1.0 + jnp.tanh(half)) * up
        act_ref[:, j * COL_CHUNK:(j + 1) * COL_CHUNK] = act.astype(BF16)
    y = x + _dot(act_ref[...], wdn_ref[...])
    if has_final:
        y = _rms_scale(y) * fg_ref[...]
    o_ref[0] = y


def _ffn(x, gain, w_up, conv_w, conv_b, w_down, att=None, w_o=None, final_gain=None):
    b, s, d = x.shape
    f = w_down.shape[0]
    tm = ROW_TILE
    has_attn, has_final = att is not None, final_gain is not None
    args, specs = [x], [_row_spec(tm, d)]
    if has_attn:
        args += [att, w_o.astype(BF16)]
        specs += [pl.BlockSpec((1, d // LANES, tm, LANES), lambda bb, i: (bb, 0, i, 0)), _resident((d, d))]
    args += [gain.reshape(1, d), w_up.astype(BF16), conv_w, conv_b.reshape(1, 2 * f), w_down.astype(BF16)]
    specs += [_resident((1, d)), _resident((d, 2 * f)), _resident((CONV_WIDTH, 2 * f)),
              _resident((1, 2 * f)), _resident((f, d))]
    if has_final:
        args.append(final_gain.reshape(1, d))
        specs.append(_resident((1, d)))
    return pl.pallas_call(
        functools.partial(_ffn_kernel, has_attn=has_attn, has_final=has_final),
        grid=(b, s // tm),
        in_specs=specs,
        out_specs=_row_spec(tm, d),
        out_shape=jax.ShapeDtypeStruct((b, s, d), F32),
        scratch_shapes=[pltpu.VMEM((SUBLANES, 2 * f), F32), pltpu.VMEM((tm, f), BF16)],
        compiler_params=_DENSE_PARAMS,
        name="ffn_attn" if has_attn else "ffn",
    )(*args)


DMAJ = 16


def _kvq_kernel(x_ref, gkv_ref, gq_ref, wkv_ref, wq_ref, *refs, scale):
    (q_nat, k_nat, v_nat), (q_maj, k_maj, v_maj) = refs[0:3], refs[3:6]
    tok_ref, mid_ref = refs[6:8]
    tm, d = x_ref.shape[1], x_ref.shape[2]
    step = 4
    y = _rms_scale(x_ref[0])
    xkv = (y * gkv_ref[...]).astype(BF16)
    xq = (y * gq_ref[...]).astype(BF16)
    head0 = lax.broadcasted_iota(jnp.int32, (1, LANES), 1) < LANES // 2
    slab = [0]

    def emit(val, pair, nat_ref, maj_ref):
        nat_ref[0, pair] = val.astype(BF16)
        i = slab[0] = (slab[0] + 1) % tok_ref.shape[0]
        tok_ref[i] = val
        for r in range(step):
            mid_ref[i, r * (tm // step):(r + 1) * (tm // step), :] = tok_ref[i, pl.ds(r, tm // step, stride=step), :]
        for r in range(DMAJ):
            rows = pl.ds((r % step) * (tm // step) + r // step, tm // DMAJ, stride=step)
            maj_ref[0, pair, 0, r] = mid_ref[i, rows, :].astype(BF16)

    for j in range(d // COL_CHUNK):
        qc = _dot(xq, wq_ref[:, j * COL_CHUNK:(j + 1) * COL_CHUNK]) * scale
        kc = _dot(xkv, wkv_ref[:, j * COL_CHUNK:(j + 1) * COL_CHUNK])
        vc = _dot(xkv, wkv_ref[:, d + j * COL_CHUNK:d + (j + 1) * COL_CHUNK])
        for h in range(COL_CHUNK // LANES):
            pair = j * (COL_CHUNK // LANES) + h
            part = slice(h * LANES, (h + 1) * LANES)
            emit(qc[:, part], pair, q_nat, q_maj)
            emit(kc[:, part], pair, k_nat, k_maj)
            emit(vc[:, part], pair, v_nat, v_maj)


def _kvq(x, kv_gain, q_gain, w_kv, w_q, scale):
    b, s, d = x.shape
    tm = ROW_TILE
    pairs = d // LANES
    nat = jax.ShapeDtypeStruct((b, pairs, s, LANES), BF16)
    maj = jax.ShapeDtypeStruct((b, pairs, s // tm, DMAJ, tm // DMAJ, LANES), BF16)
    nat_spec = pl.BlockSpec((1, pairs, tm, LANES), lambda bb, i: (bb, 0, i, 0))
    maj_spec = pl.BlockSpec((1, pairs, 1, DMAJ, tm // DMAJ, LANES), lambda bb, i: (bb, 0, i, 0, 0, 0))
    return pl.pallas_call(
        functools.partial(_kvq_kernel, scale=scale),
        grid=(b, s // tm),
        in_specs=[_row_spec(tm, d), _resident((1, d)), _resident((1, d)),
                  _resident((d, 2 * d)), _resident((d, d))],
        out_specs=[nat_spec] * 3 + [maj_spec] * 3,
        out_shape=[nat] * 3 + [maj] * 3,
        scratch_shapes=[pltpu.VMEM((8, tm, LANES), F32)] * 2,
        compiler_params=_DENSE_PARAMS,
        name="kvq",
    )(x, kv_gain.reshape(1, d), q_gain.reshape(1, d), w_kv.astype(BF16), w_q.astype(BF16))


def _t5_bucket(dist):
    max_exact = REL_BUCKETS // 2
    n = jnp.maximum(dist, 0)
    nf = jnp.maximum(n, max_exact).astype(jnp.float32)
    large = max_exact + (jnp.log(nf / max_exact) / math.log(REL_MAX_DISTANCE / max_exact)
                         * (REL_BUCKETS - max_exact)).astype(jnp.int32)
    large = jnp.minimum(large, REL_BUCKETS - 1)
    return jnp.where(n < max_exact, n, large)


def _bias_table(rel_bias, qpos, kpos, dilation, window):
    delta = qpos[:, None] - kpos[None, :]
    band = (delta >= 0) & (delta <= window // dilation)
    onehot = jax.nn.one_hot(_t5_bucket(delta * dilation), REL_BUCKETS, dtype=F32)
    bias = jnp.einsum("qkn,nh->hqk", onehot, rel_bias.astype(F32), precision=lax.Precision.HIGHEST)
    return jnp.where(band[None], bias * LOG2E, MASKED)


def _bias_tables(rel_bias):
    p = BLOCK
    (w1, d1), (w4, d4), (w16, d16) = DILATED_BRANCHES
    runs = DMAJ // d4
    ar = lambda n: jnp.arange(n, dtype=jnp.int32)
    q_perm = (ar(p) % (p // runs)) * runs + ar(p) // (p // runs)
    k_perm_wide = (ar(2 * p) % (2 * p // runs)) * runs + ar(2 * p) // (2 * p // runs) - p
    wide = [
        _bias_table(rel_bias, ar(p), ar(2 * p) - p, d1, w1),
        _bias_table(rel_bias, ar(p), ar(2 * p), d1, w1),
        _bias_table(rel_bias, q_perm, k_perm_wide, d4, w4),
        _bias_table(rel_bias, ar(p), ar(2 * p) - p, d16, w16),
    ]
    narrow = [
        _bias_table(rel_bias, q_perm, q_perm, d4, w4),
        _bias_table(rel_bias, ar(p), ar(p), d16, w16),
    ]

    def pack(tabs):
        t = jnp.stack(tabs, axis=0)
        t = t.reshape(t.shape[0], N_HEADS // 2, 2, p, t.shape[-1])
        return t.transpose(1, 0, 2, 3, 4)

    return pack(wide), pack(narrow)


D1_INNER, D1_FIRST, D4_INNER, D16_INNER = range(4)
D4_FIRST, D16_FIRST = range(2)
ATTN_SLOTS = 2
ATTN_GROUP = 4


def _attn_kernel(zero_ref, qn_ref, kn_ref, vn_ref, qm_ref, km_ref, vm_ref, wide_ref, narrow_ref,
                 o_ref, acc_ref, den_ref, mx_ref, out_ref, s_ring, p_ring):
    s = qn_ref.shape[1]
    p = BLOCK
    (_, d1), (_, d4), (_, d16) = DILATED_BRANCHES
    runs = DMAJ // d4
    rl = p // runs
    quarter = s // d4
    head0 = lax.broadcasted_iota(jnp.int32, (1, LANES), 1) < LANES // 2
    one = jnp.ones((), BF16)

    zero = zero_ref[0]

    def ring(slot, u):
        return slot * 2 * ATTN_GROUP + u + zero

    def score_stage(blocks, slot):
        for i, (q, k, _, tab_ref, kind, _, _) in enumerate(blocks):
            qb, kw = q(), k()
            for hh in range(2):
                qh = jnp.where(head0 if hh == 0 else jnp.logical_not(head0), qb, jnp.zeros_like(qb))
                sc = lax.dot_general(qh, kw, (((1,), (1,)), ((), ())), preferred_element_type=F32)
                s_ring[ring(slot, 2 * i + hh), :, 0:kw.shape[0]] = sc + tab_ref[0, kind, hh]

    def softmax_stage(blocks, slot):
        for i, (_, k, _, _, _, put, _) in enumerate(blocks):
            maxes = []
            for u in (2 * i, 2 * i + 1):
                sc = s_ring[ring(slot, u), :, 0:k.width]
                m = jnp.max(sc, axis=-1, keepdims=True)
                p_ring[ring(slot, u), :, 0:k.width] = jnp.exp2((sc - m).astype(BF16))
                maxes.append(jnp.broadcast_to(m, (p, LANES)))
            put(mx_ref, jnp.where(head0, maxes[0], maxes[1]))

    def value_stage(blocks, slot):
        for i, (_, k, v, _, _, put, after) in enumerate(blocks):
            vw = v()
            o0 = _dot(p_ring[ring(slot, 2 * i), :, 0:k.width], jnp.where(head0, vw, one))
            o1 = _dot(p_ring[ring(slot, 2 * i + 1), :, 0:k.width], jnp.where(head0, one, vw))
            put(acc_ref, jnp.where(head0, o0, o1))
            put(den_ref, pltpu.roll(jnp.where(head0, o1, o0), LANES // 2, 1))
            if after is not None:
                after()

    def window(fn, width):
        fn.width = width
        return fn

    def maj_rows(ref, r, lo, n):
        tile_len = ref.shape[3]
        pieces = []
        while n > 0:
            t, off = divmod(lo, tile_len)
            m = min(n, tile_len - off)
            pieces.append(ref[0, t, r, off:off + m, :])
            lo, n = lo + m, n - m
        return pieces[0] if len(pieces) == 1 else jnp.concatenate(pieces, axis=0)

    def put_rows(bi, rows):
        def put(ref, val):
            ref[bi, rows, :] = val
        return put

    blocks1, blocks16, blocks4 = [], [], {}
    for g in range(s // p):
        row0, win = g * p, slice(max(g - 1, 0) * p, max(g - 1, 0) * p + 2 * p)
        blocks1.append((
            lambda row0=row0: qn_ref[0, row0:row0 + p, :],
            window(lambda win=win: kn_ref[0, win, :], 2 * p),
            lambda win=win: vn_ref[0, win, :],
            wide_ref, D1_FIRST if g == 0 else D1_INNER, put_rows(0, pl.ds(row0, p)), None))
    for r in range(d16):
        base = (r % d4) * quarter + r // d4
        for j in range(s // d16 // p):
            win = slice(max(j - 1, 0) * p, (j + 1) * p)
            blocks16.append((
                lambda r=r, j=j: maj_rows(qm_ref, r, j * p, p),
                window(lambda r=r, win=win: maj_rows(km_ref, r, win.start, win.stop - win.start),
                       win.stop - win.start),
                lambda r=r, win=win: maj_rows(vm_ref, r, win.start, win.stop - win.start),
                narrow_ref if j == 0 else wide_ref, D16_FIRST if j == 0 else D16_INNER,
                put_rows(2, pl.ds(base + j * p * (d16 // d4), p, stride=d16 // d4)), None))
    for r4 in range(d4):
        def gather(ref, lo, n, r4=r4):
            return jnp.concatenate([maj_rows(ref, c * d4 + r4, lo, n) for c in range(runs)], axis=0)

        def put_runs(j, r4=r4):
            def put(ref, val):
                for c in range(runs):
                    ref[1, pl.ds(r4 * quarter + j * p + c, rl, stride=runs), :] = val[c * rl:(c + 1) * rl]
            return put

        for j in range(s // d4 // p):
            lo = max(j - 1, 0) * rl
            width = rl if j == 0 else 2 * rl
            blocks4[r4, j] = (
                lambda j=j, gather=gather: gather(qm_ref, j * rl, rl),
                window(lambda lo=lo, width=width, gather=gather: gather(km_ref, lo, width), runs * width),
                lambda lo=lo, width=width, gather=gather: gather(vm_ref, lo, width),
                narrow_ref if j == 0 else wide_ref, D4_FIRST if j == 0 else D4_INNER, put_runs(j), None)

    blocks = blocks1 + blocks16 + [blocks4[r4, j] for r4 in range(d4) for j in range(s // d4 // p)]

    pending = []
    for n, g0 in enumerate(range(0, len(blocks), ATTN_GROUP)):
        group = blocks[g0:g0 + ATTN_GROUP]
        score_stage(group, n % ATTN_SLOTS)
        if len(pending) == ATTN_SLOTS - 1:
            value_stage(*pending.pop(0))
        softmax_stage(group, n % ATTN_SLOTS)
        pending.append((group, n % ATTN_SLOTS))
    for item in pending:
        value_stage(*item)

    mt = 4 * p
    per_quarter = quarter // mt

    def merge_body(i, carry):
        r4 = i // per_quarter
        a = (i % per_quarter) * mt
        rows_maj = pl.ds(pl.multiple_of(r4 * quarter + a, mt), mt)
        rows_tok = pl.ds(a * d4 + r4, mt, stride=d4)
        rows = (rows_tok, rows_maj, rows_maj)
        ms = [mx_ref[bi, rows[bi], :] for bi in range(3)]
        m_all = functools.reduce(jnp.maximum, ms)
        ws = [jnp.exp2(m - m_all) for m in ms]
        num = functools.reduce(jnp.add, [w * acc_ref[bi, rows[bi], :] for bi, w in enumerate(ws)])
        den = functools.reduce(jnp.add, [w * den_ref[bi, rows[bi], :] for bi, w in enumerate(ws)])
        out_ref[rows_tok, :] = num / den
        return carry

    lax.fori_loop(0, s // mt, merge_body, 0)

    def cast_body(i, carry):
        rows = pl.ds(pl.multiple_of(i * mt, mt), mt)
        o_ref[0, rows, :] = out_ref[rows, :].astype(o_ref.dtype)
        return carry

    lax.fori_loop(0, s // mt, cast_body, 0)


def _attention(nat, maj, wide, narrow):
    b, pairs, s, _ = nat[0].shape
    d = pairs * LANES
    nat_spec = pl.BlockSpec((1, None, s, LANES), lambda hp, bb: (bb, hp, 0, 0))
    maj_spec = pl.BlockSpec((1, None) + maj[0].shape[2:], lambda hp, bb: (bb, hp, 0, 0, 0, 0))
    tab_spec = lambda t: pl.BlockSpec((1,) + t.shape[1:], lambda hp, bb: (hp, 0, 0, 0, 0))
    per_branch = pltpu.VMEM((len(DILATED_BRANCHES), s, LANES), F32)
    return pl.pallas_call(
        _attn_kernel,
        grid=(d // LANES, b),
        in_specs=[pl.BlockSpec(memory_space=pltpu.SMEM)] + [nat_spec] * 3 + [maj_spec] * 3
        + [tab_spec(wide), tab_spec(narrow)],
        out_specs=nat_spec,
        out_shape=jax.ShapeDtypeStruct((b, pairs, s, LANES), BF16),
        scratch_shapes=[per_branch] * 3 + [
            pltpu.VMEM((s, LANES), F32),
            pltpu.VMEM((ATTN_SLOTS * 2 * ATTN_GROUP, BLOCK, 2 * BLOCK), F32),
            pltpu.VMEM((ATTN_SLOTS * 2 * ATTN_GROUP, BLOCK, 2 * BLOCK), BF16),
        ],
        compiler_params=pltpu.CompilerParams(
            dimension_semantics=("arbitrary", "arbitrary"), vmem_limit_bytes=VMEM_LIMIT),
        name="dilated_attn",
    )(jnp.zeros((1,), jnp.int32), *nat, *maj, wide, narrow)


def kernel(x, a_norm, a_w_in, a_conv, a_w_out, kv_norm, w_kv, b_norm, b_w_q, b_w_o, rel_bias,
           ffn_norm, ffn_w_up, ffn_conv, ffn_conv_b, ffn_w_down, final_norm):
    d = x.shape[-1]
    n_a, n_b = a_norm.shape[0], b_norm.shape[0]
    depth = n_a + n_b
    assert n_b == 1, "the shared K/V and the query projection are fused for a single attention layer"
    scale = (d // N_HEADS) ** -0.5 * LOG2E
    wide, narrow = _bias_tables(rel_bias)

    h = x
    for l in range(n_a):
        h = _mixer(h, a_norm[l], a_w_in[l], a_conv[l], a_w_out[l])
        h = _ffn(h, ffn_norm[l], ffn_w_up[l], ffn_conv[l], ffn_conv_b[l], ffn_w_down[l])
    outs = _kvq(h, kv_norm, b_norm[0], w_kv, b_w_q[0], scale)
    att = _attention(outs[:3], outs[3:], wide, narrow)
    l = depth - 1
    return _ffn(h, ffn_norm[l], ffn_w_up[l], ffn_conv[l], ffn_conv_b[l], ffn_w_down[l],
                att=att, w_o=b_w_o[0], final_gain=final_norm)
```

```python
import functools
import math

import jax
import jax.numpy as jnp
from jax import lax
from jax.experimental import pallas as pl
from jax.experimental.pallas import tpu as pltpu

N_HEADS = 16
CONV_WIDTH = 3
DILATED_BRANCHES = ((128, 1), (512, 4), (2048, 16))
BLOCK = 128
REL_BUCKETS = 32
REL_MAX_DISTANCE = 2048
RMS_EPS = 1e-6

LANES = 128
SUBLANES = 8
ROW_TILE = 1024
COL_CHUNK = 256
VMEM_LIMIT = 56 * 1024 * 1024
MASKED = -1e30
LOG2E = math.log2(math.e)

BF16 = jnp.bfloat16
F32 = jnp.float32


def _dot(a, b):
    return jnp.dot(a, b, preferred_element_type=F32)


def _rms_scale(x):
    return x * lax.rsqrt(jnp.mean(x * x, axis=-1, keepdims=True) + RMS_EPS)


def _shift_rows(cur, prev, k):
    rolled = pltpu.roll(cur, k, 0)
    top = pltpu.roll(prev, k, 0)
    row = lax.broadcasted_iota(jnp.int32, prev.shape, 0)
    first = jnp.where(row < k, top, rolled[:SUBLANES])
    return jnp.concatenate([first, rolled[SUBLANES:]], axis=0)


def _causal_conv3(u, carry_ref, w_ref, cols):
    prev = carry_ref[:, cols]
    carry_ref[:, cols] = u[u.shape[0] - SUBLANES:]
    u1 = _shift_rows(u, prev, 1)
    u2 = _shift_rows(u, prev, 2)
    return w_ref[0:1, cols] * u2 + w_ref[1:2, cols] * u1 + w_ref[2:3, cols] * u


def _resident(shape):
    return pl.BlockSpec(shape, lambda b, i: (0,) * len(shape), pipeline_mode=pl.Buffered(1))


def _row_spec(tm, width):
    return pl.BlockSpec((1, tm, width), lambda b, i: (b, i, 0))


_DENSE_PARAMS = pltpu.CompilerParams(
    dimension_semantics=("arbitrary", "arbitrary"), vmem_limit_bytes=VMEM_LIMIT)


def _mixer_kernel(x_ref, g_ref, win_ref, cw_ref, wout_ref, o_ref, carry_ref, y_ref):
    d = x_ref.shape[2]

    @pl.when(pl.program_id(1) == 0)
    def _():
        carry_ref[...] = jnp.zeros_like(carry_ref)

    x = x_ref[0]
    xn = (_rms_scale(x) * g_ref[...]).astype(BF16)

    for j in range(d // COL_CHUNK):
        cols = slice(j * COL_CHUNK, (j + 1) * COL_CHUNK)
        gate_b = _dot(xn, win_ref[:, cols])
        gate_c = _dot(xn, win_ref[:, d + j * COL_CHUNK:d + (j + 1) * COL_CHUNK])
        hh = _dot(xn, win_ref[:, 2 * d + j * COL_CHUNK:2 * d + (j + 1) * COL_CHUNK])
        ch = gate_c * hh
        conv = _causal_conv3(ch, carry_ref, cw_ref, cols)
        y_ref[:, cols] = (gate_b * conv).astype(BF16)
    o_ref[0] = x + _dot(y_ref[...], wout_ref[...])


def _mixer(x, gain, w_in, conv_w, w_out):
    b, s, d = x.shape
    tm = ROW_TILE
    return pl.pallas_call(
        _mixer_kernel,
        grid=(b, s // tm),
        in_specs=[_row_spec(tm, d), _resident((1, d)), _resident((d, 3 * d)),
                  _resident((CONV_WIDTH, d)), _resident((d, d))],
        out_specs=_row_spec(tm, d),
        out_shape=jax.ShapeDtypeStruct((b, s, d), F32),
        scratch_shapes=[pltpu.VMEM((SUBLANES, d), F32), pltpu.VMEM((tm, d), BF16)],
        compiler_params=_DENSE_PARAMS,
        name="mixer",
    )(x, gain.reshape(1, d), w_in.astype(BF16), conv_w, w_out.astype(BF16))


def _ffn_kernel(*refs, has_attn, has_final):
    refs = list(refs)
    x_ref = refs.pop(0)
    att_ref, wo_ref = (refs.pop(0), refs.pop(0)) if has_attn else (None, None)
    g_ref, wup_ref, cw_ref, cb_ref, wdn_ref = (refs.pop(0) for _ in range(5))
    fg_ref = refs.pop(0) if has_final else None
    o_ref, carry_ref, act_ref = refs
    f = wdn_ref.shape[0]

    @pl.when(pl.program_id(1) == 0)
    def _():
        carry_ref[...] = jnp.zeros_like(carry_ref)

    x = x_ref[0]
    if has_attn:
        att = jnp.concatenate([att_ref[0, hp] for hp in range(att_ref.shape[1])], axis=1)
        x = x + _dot(att, wo_ref[...])
    xn = (_rms_scale(x) * g_ref[...]).astype(BF16)

    def conv_chunk(c0):
        cols = slice(c0, c0 + COL_CHUNK)
        u = _dot(xn, wup_ref[:, cols])
        return _causal_conv3(u, carry_ref, cw_ref, cols) + cb_ref[:, cols]

    for j in range(f // COL_CHUNK):
        gate = conv_chunk(j * COL_CHUNK)
        up = conv_chunk(f + j * COL_CHUNK)
        half = 0.5 * gate
        act = half * (1.0 + jnp.tanh(half)) * up
        act_ref[:, j * COL_CHUNK:(j + 1) * COL_CHUNK] = act.astype(BF16)
    y = x + _dot(act_ref[...], wdn_ref[...])
    if has_final:
        y = _rms_scale(y) * fg_ref[...]
    o_ref[0] = y


def _ffn(x, gain, w_up, conv_w, conv_b, w_down, att=None, w_o=None, final_gain=None):
    b, s, d = x.shape
    f = w_down.shape[0]
    tm = ROW_TILE
    has_attn, has_final = att is not None, final_gain is not None
    args, specs = [x], [_row_spec(tm, d)]
    if has_attn:
        args += [att, w_o.astype(BF16)]
        specs += [pl.BlockSpec((1, d // LANES, tm, LANES), lambda bb, i: (bb, 0, i, 0)), _resident((d, d))]
    args += [gain.reshape(1, d), w_up.astype(BF16), conv_w, conv_b.reshape(1, 2 * f), w_down.astype(BF16)]
    specs += [_resident((1, d)), _resident((d, 2 * f)), _resident((CONV_WIDTH, 2 * f)),
              _resident((1, 2 * f)), _resident((f, d))]
    if has_final:
        args.append(final_gain.reshape(1, d))
        specs.append(_resident((1, d)))
    return pl.pallas_call(
        functools.partial(_ffn_kernel, has_attn=has_attn, has_final=has_final),
        grid=(b, s // tm),
        in_specs=specs,
        out_specs=_row_spec(tm, d),
        out_shape=jax.ShapeDtypeStruct((b, s, d), F32),
        scratch_shapes=[pltpu.VMEM((SUBLANES, 2 * f), F32), pltpu.VMEM((tm, f), BF16)],
        compiler_params=_DENSE_PARAMS,
        name="ffn_attn" if has_attn else "ffn",
    )(*args)


DMAJ = 16


def _kvq_kernel(x_ref, gkv_ref, gq_ref, wkv_ref, wq_ref, *refs, scale):
    (q_nat, k_nat, v_nat), (q_maj, k_maj, v_maj) = refs[0:3], refs[3:6]
    tok_ref, mid_ref = refs[6:8]
    tm, d = x_ref.shape[1], x_ref.shape[2]
    step = 4
    y = _rms_scale(x_ref[0])
    xkv = (y * gkv_ref[...]).astype(BF16)
    xq = (y * gq_ref[...]).astype(BF16)
    head0 = lax.broadcasted_iota(jnp.int32, (1, LANES), 1) < LANES // 2
    slab = [0]

    def emit(val, pair, nat_ref, maj_ref):
        nat_ref[0, pair] = val.astype(BF16)
        i = slab[0] = (slab[0] + 1) % tok_ref.shape[0]
        tok_ref[i] = val
        for r in range(step):
            mid_ref[i, r * (tm // step):(r + 1) * (tm // step), :] = tok_ref[i, pl.ds(r, tm // step, stride=step), :]
        for r in range(DMAJ):
            rows = pl.ds((r % step) * (tm // step) + r // step, tm // DMAJ, stride=step)
            maj_ref[0, pair, 0, r] = mid_ref[i, rows, :].astype(BF16)

    for j in range(d // COL_CHUNK):
        qc = _dot(xq, wq_ref[:, j * COL_CHUNK:(j + 1) * COL_CHUNK]) * scale
        kc = _dot(xkv, wkv_ref[:, j * COL_CHUNK:(j + 1) * COL_CHUNK])
        vc = _dot(xkv, wkv_ref[:, d + j * COL_CHUNK:d + (j + 1) * COL_CHUNK])
        for h in range(COL_CHUNK // LANES):
            pair = j * (COL_CHUNK // LANES) + h
            part = slice(h * LANES, (h + 1) * LANES)
            emit(qc[:, part], pair, q_nat, q_maj)
            emit(kc[:, part], pair, k_nat, k_maj)
            emit(vc[:, part], pair, v_nat, v_maj)


def _kvq(x, kv_gain, q_gain, w_kv, w_q, scale):
    b, s, d = x.shape
    tm = ROW_TILE
    pairs = d // LANES
    nat = jax.ShapeDtypeStruct((b, pairs, s, LANES), BF16)
    maj = jax.ShapeDtypeStruct((b, pairs, s // tm, DMAJ, tm // DMAJ, LANES), BF16)
    nat_spec = pl.BlockSpec((1, pairs, tm, LANES), lambda bb, i: (bb, 0, i, 0))
    maj_spec = pl.BlockSpec((1, pairs, 1, DMAJ, tm // DMAJ, LANES), lambda bb, i: (bb, 0, i, 0, 0, 0))
    return pl.pallas_call(
        functools.partial(_kvq_kernel, scale=scale),
        grid=(b, s // tm),
        in_specs=[_row_spec(tm, d), _resident((1, d)), _resident((1, d)),
                  _resident((d, 2 * d)), _resident((d, d))],
        out_specs=[nat_spec] * 3 + [maj_spec] * 3,
        out_shape=[nat] * 3 + [maj] * 3,
        scratch_shapes=[pltpu.VMEM((8, tm, LANES), F32)] * 2,
        compiler_params=_DENSE_PARAMS,
        name="kvq",
    )(x, kv_gain.reshape(1, d), q_gain.reshape(1, d), w_kv.astype(BF16), w_q.astype(BF16))


def _t5_bucket(dist):
    max_exact = REL_BUCKETS // 2
    n = jnp.maximum(dist, 0)
    nf = jnp.maximum(n, max_exact).astype(jnp.float32)
    large = max_exact + (jnp.log(nf / max_exact) / math.log(REL_MAX_DISTANCE / max_exact)
                         * (REL_BUCKETS - max_exact)).astype(jnp.int32)
    large = jnp.minimum(large, REL_BUCKETS - 1)
    return jnp.where(n < max_exact, n, large)


def _bias_table(rel_bias, qpos, kpos, dilation, window):
    delta = qpos[:, None] - kpos[None, :]
    band = (delta >= 0) & (delta <= window // dilation)
    onehot = jax.nn.one_hot(_t5_bucket(delta * dilation), REL_BUCKETS, dtype=F32)
    bias = jnp.einsum("qkn,nh->hqk", onehot, rel_bias.astype(F32), precision=lax.Precision.HIGHEST)
    return jnp.where(band[None], bias * LOG2E, MASKED)


def _bias_tables(rel_bias):
    p = BLOCK
    (w1, d1), (w4, d4), (w16, d16) = DILATED_BRANCHES
    runs = DMAJ // d4
    ar = lambda n: jnp.arange(n, dtype=jnp.int32)
    q_perm = (ar(p) % (p // runs)) * runs + ar(p) // (p // runs)
    k_perm_wide = (ar(2 * p) % (2 * p // runs)) * runs + ar(2 * p) // (2 * p // runs) - p
    wide = [
        _bias_table(rel_bias, ar(p), ar(2 * p) - p, d1, w1),
        _bias_table(rel_bias, ar(p), ar(2 * p), d1, w1),
        _bias_table(rel_bias, q_perm, k_perm_wide, d4, w4),
        _bias_table(rel_bias, ar(p), ar(2 * p) - p, d16, w16),
    ]
    narrow = [
        _bias_table(rel_bias, q_perm, q_perm, d4, w4),
        _bias_table(rel_bias, ar(p), ar(p), d16, w16),
    ]

    def pack(tabs):
        t = jnp.stack(tabs, axis=0)
        t = t.reshape(t.shape[0], N_HEADS // 2, 2, p, t.shape[-1])
        return t.transpose(1, 0, 2, 3, 4)

    return pack(wide), pack(narrow)


D1_INNER, D1_FIRST, D4_INNER, D16_INNER = range(4)
D4_FIRST, D16_FIRST = range(2)
ATTN_SLOTS = 2
ATTN_GROUP = 4


def _attn_kernel(zero_ref, qn_ref, kn_ref, vn_ref, qm_ref, km_ref, vm_ref, wide_ref, narrow_ref,
                 o_ref, acc_ref, den_ref, mx_ref, out_ref, s_ring, p_ring, van_ref, vbn_ref, vam_ref, vbm_ref):
    s = qn_ref.shape[1]
    p = BLOCK
    (_, d1), (_, d4), (_, d16) = DILATED_BRANCHES
    runs = DMAJ // d4
    rl = p // runs
    quarter = s // d4
    head0 = lax.broadcasted_iota(jnp.int32, (1, LANES), 1) < LANES // 2
    one = jnp.ones((), BF16)
    for src, dst_a, dst_b in ((vn_ref, van_ref, vbn_ref), (vm_ref, vam_ref, vbm_ref)):
        dst_a[...] = jnp.where(head0, src[...], one)
        dst_b[...] = jnp.where(head0, one, src[...])

    zero = zero_ref[0]

    def ring(slot, u):
        return slot * 2 * ATTN_GROUP + u + zero

    def score_stage(blocks, slot):
        for i, (q, k, _, tab_ref, kind, _, _) in enumerate(blocks):
            qb, kw = q(), k()
            for hh in range(2):
                qh = jnp.where(head0 if hh == 0 else jnp.logical_not(head0), qb, jnp.zeros_like(qb))
                sc = lax.dot_general(qh, kw, (((1,), (1,)), ((), ())), preferred_element_type=F32)
                s_ring[ring(slot, 2 * i + hh), :, 0:kw.shape[0]] = sc + tab_ref[0, kind, hh]

    def softmax_stage(blocks, slot):
        for i, (_, k, _, _, _, put, _) in enumerate(blocks):
            maxes = []
            for u in (2 * i, 2 * i + 1):
                sc = s_ring[ring(slot, u), :, 0:k.width]
                m = jnp.max(sc, axis=-1, keepdims=True)
                p_ring[ring(slot, u), :, 0:k.width] = jnp.exp2((sc - m).astype(BF16))
                maxes.append(jnp.broadcast_to(m, (p, LANES)))
            put(mx_ref, jnp.where(head0, maxes[0], maxes[1]))

    def value_stage(blocks, slot):
        for i, (_, k, v, _, _, put, after) in enumerate(blocks):
            o0 = _dot(p_ring[ring(slot, 2 * i), :, 0:k.width], v(van_ref, vam_ref))
            o1 = _dot(p_ring[ring(slot, 2 * i + 1), :, 0:k.width], v(vbn_ref, vbm_ref))
            put(acc_ref, jnp.where(head0, o0, o1))
            put(den_ref, pltpu.roll(jnp.where(head0, o1, o0), LANES // 2, 1))
            if after is not None:
                after()

    def window(fn, width):
        fn.width = width
        return fn

    def maj_rows(ref, r, lo, n):
        tile_len = ref.shape[3]
        pieces = []
        while n > 0:
            t, off = divmod(lo, tile_len)
            m = min(n, tile_len - off)
            pieces.append(ref[0, t, r, off:off + m, :])
            lo, n = lo + m, n - m
        return pieces[0] if len(pieces) == 1 else jnp.concatenate(pieces, axis=0)

    def put_rows(bi, rows):
        def put(ref, val):
            ref[bi, rows, :] = val
        return put

    blocks1, blocks16, blocks4 = [], [], {}
    for g in range(s // p):
        row0, win = g * p, slice(max(g - 1, 0) * p, max(g - 1, 0) * p + 2 * p)
        blocks1.append((
            lambda row0=row0: qn_ref[0, row0:row0 + p, :],
            window(lambda win=win: kn_ref[0, win, :], 2 * p),
            lambda n, m, win=win: n[0, win, :],
            wide_ref, D1_FIRST if g == 0 else D1_INNER, put_rows(0, pl.ds(row0, p)), None))
    for r in range(d16):
        base = (r % d4) * quarter + r // d4
        for j in range(s // d16 // p):
            win = slice(max(j - 1, 0) * p, (j + 1) * p)
            blocks16.append((
                lambda r=r, j=j: maj_rows(qm_ref, r, j * p, p),
                window(lambda r=r, win=win: maj_rows(km_ref, r, win.start, win.stop - win.start),
                       win.stop - win.start),
                lambda n, m, r=r, win=win: maj_rows(m, r, win.start, win.stop - win.start),
                narrow_ref if j == 0 else wide_ref, D16_FIRST if j == 0 else D16_INNER,
                put_rows(2, pl.ds(base + j * p * (d16 // d4), p, stride=d16 // d4)), None))
    for r4 in range(d4):
        def gather(ref, lo, n, r4=r4):
            return jnp.concatenate([maj_rows(ref, c * d4 + r4, lo, n) for c in range(runs)], axis=0)

        def put_runs(j, r4=r4):
            def put(ref, val):
                for c in range(runs):
                    ref[1, pl.ds(r4 * quarter + j * p + c, rl, stride=runs), :] = val[c * rl:(c + 1) * rl]
            return put

        for j in range(s // d4 // p):
            lo = max(j - 1, 0) * rl
            width = rl if j == 0 else 2 * rl
            blocks4[r4, j] = (
                lambda j=j, gather=gather: gather(qm_ref, j * rl, rl),
                window(lambda lo=lo, width=width, gather=gather: gather(km_ref, lo, width), runs * width),
                lambda n, m, lo=lo, width=width, gather=gather: gather(m, lo, width),
                narrow_ref if j == 0 else wide_ref, D4_FIRST if j == 0 else D4_INNER, put_runs(j), None)

    blocks = blocks1 + blocks16 + [blocks4[r4, j] for r4 in range(d4) for j in range(s // d4 // p)]

    pending = []
    for n, g0 in enumerate(range(0, len(blocks), ATTN_GROUP)):
        group = blocks[g0:g0 + ATTN_GROUP]
        score_stage(group, n % ATTN_SLOTS)
        if len(pending) == ATTN_SLOTS - 1:
            value_stage(*pending.pop(0))
        softmax_stage(group, n % ATTN_SLOTS)
        pending.append((group, n % ATTN_SLOTS))
    for item in pending:
        value_stage(*item)

    mt = 4 * p
    per_quarter = quarter // mt

    def merge_body(i, carry):
        r4 = i // per_quarter
        a = (i % per_quarter) * mt
        rows_maj = pl.ds(pl.multiple_of(r4 * quarter + a, mt), mt)
        rows_tok = pl.ds(a * d4 + r4, mt, stride=d4)
        rows = (rows_tok, rows_maj, rows_maj)
        ms = [mx_ref[bi, rows[bi], :] for bi in range(3)]
        m_all = functools.reduce(jnp.maximum, ms)
        ws = [jnp.exp2(m - m_all) for m in ms]
        num = functools.reduce(jnp.add, [w * acc_ref[bi, rows[bi], :] for bi, w in enumerate(ws)])
        den = functools.reduce(jnp.add, [w * den_ref[bi, rows[bi], :] for bi, w in enumerate(ws)])
        out_ref[rows_tok, :] = num / den
        return carry

    lax.fori_loop(0, s // mt, merge_body, 0)

    def cast_body(i, carry):
        rows = pl.ds(pl.multiple_of(i * mt, mt), mt)
        o_ref[0, rows, :] = out_ref[rows, :].astype(o_ref.dtype)
        return carry

    lax.fori_loop(0, s // mt, cast_body, 0)


def _attention(nat, maj, wide, narrow):
    b, pairs, s, _ = nat[0].shape
    d = pairs * LANES
    nat_spec = pl.BlockSpec((1, None, s, LANES), lambda hp, bb: (bb, hp, 0, 0))
    maj_spec = pl.BlockSpec((1, None) + maj[0].shape[2:], lambda hp, bb: (bb, hp, 0, 0, 0, 0))
    tab_spec = lambda t: pl.BlockSpec((1,) + t.shape[1:], lambda hp, bb: (hp, 0, 0, 0, 0))
    per_branch = pltpu.VMEM((len(DILATED_BRANCHES), s, LANES), F32)
    return pl.pallas_call(
        _attn_kernel,
        grid=(d // LANES, b),
        in_specs=[pl.BlockSpec(memory_space=pltpu.SMEM)] + [nat_spec] * 3 + [maj_spec] * 3
        + [tab_spec(wide), tab_spec(narrow)],
        out_specs=nat_spec,
        out_shape=jax.ShapeDtypeStruct((b, pairs, s, LANES), BF16),
        scratch_shapes=[per_branch] * 3 + [
            pltpu.VMEM((s, LANES), F32),
            pltpu.VMEM((ATTN_SLOTS * 2 * ATTN_GROUP, BLOCK, 2 * BLOCK), F32),
            pltpu.VMEM((ATTN_SLOTS * 2 * ATTN_GROUP, BLOCK, 2 * BLOCK), BF16),
            pltpu.VMEM((1, s, LANES), BF16), pltpu.VMEM((1, s, LANES), BF16),
            pltpu.VMEM((1,) + maj[0].shape[2:], BF16), pltpu.VMEM((1,) + maj[0].shape[2:], BF16),
        ],
        compiler_params=pltpu.CompilerParams(
            dimension_semantics=("arbitrary", "arbitrary"), vmem_limit_bytes=VMEM_LIMIT),
        name="dilated_attn",
    )(jnp.zeros((1,), jnp.int32), *nat, *maj, wide, narrow)


def kernel(x, a_norm, a_w_in, a_conv, a_w_out, kv_norm, w_kv, b_norm, b_w_q, b_w_o, rel_bias,
           ffn_norm, ffn_w_up, ffn_conv, ffn_conv_b, ffn_w_down, final_norm):
    d = x.shape[-1]
    n_a, n_b = a_norm.shape[0], b_norm.shape[0]
    depth = n_a + n_b
    assert n_b == 1, "the shared K/V and the query projection are fused for a single attention layer"
    scale = (d // N_HEADS) ** -0.5 * LOG2E
    wide, narrow = _bias_tables(rel_bias)

    h = x
    for l in range(n_a):
        h = _mixer(h, a_norm[l], a_w_in[l], a_conv[l], a_w_out[l])
        h = _ffn(h, ffn_norm[l], ffn_w_up[l], ffn_conv[l], ffn_conv_b[l], ffn_w_down[l])
    outs = _kvq(h, kv_norm, b_norm[0], w_kv, b_w_q[0], scale)
    att = _attention(outs[:3], outs[3:], wide, narrow)
    l = depth - 1
    return _ffn(h, ffn_norm[l], ffn_w_up[l], ffn_conv[l], ffn_conv_b[l], ffn_w_down[l],
                att=att, w_o=b_w_o[0], final_gain=final_norm)
```
